```python
import math
import jax, jax.numpy as jnp
from jax import lax
import numpy as np

D_MODEL = 4096
BATCH = 4
SEQ = 2048
DEPTH = 2
DEC_BATCH = 8
DEC_SEQ = 8
PAST_LEN = 16384
PAGE_SIZE = 128

MIX_WIDTH = D_MODEL
MIX_A = MIX_WIDTH // 2
MIX_B = MIX_WIDTH - MIX_A
N_HEADS_A = 8
DV_A = MIX_A // N_HEADS_A
DK_A = DV_A // 2
HEAD_DIM = 128
N_HEADS_B = MIX_B // HEAD_DIM
DILATED_PATTERNS = ((128, 1), (512, 4), (2048, 16))
MAX_WINDOW = 2048
N_BUCKETS = 32
MAX_DISTANCE = 2048
MLSTM_CHUNK = 64
D_FF = ((8 * D_MODEL // 3 + 255) // 256) * 256
EPS = 1e-6
NEG_INF = -1e30
SPLIT_SIZES = (N_HEADS_A * DK_A, N_HEADS_A * DK_A, N_HEADS_A * DV_A, N_HEADS_A * DV_A,
               N_HEADS_A, N_HEADS_A, N_HEADS_B * HEAD_DIM, N_HEADS_B * HEAD_DIM, N_HEADS_B * HEAD_DIM)
SPLIT_POINTS = tuple(int(s) for s in np.cumsum(SPLIT_SIZES)[:-1])
N_IN = int(sum(SPLIT_SIZES))

kernel_name = "hymba_mlstm_dilated_window_macaron_step"


def _rmsnorm(x, g):
    x32 = x.astype(jnp.float32)
    y = x32 * lax.rsqrt(jnp.mean(x32 * x32, axis=-1, keepdims=True) + EPS)
    return (y * g.astype(jnp.float32)).astype(x.dtype)


def _swiglu(h, wg, wu, wd):
    return (jax.nn.silu(h @ wg) * (h @ wu)) @ wd


def _t5_bucket(dist):
    max_exact = N_BUCKETS // 2
    d = np.maximum(dist, 1).astype(np.float64)
    large = max_exact + (np.log(d / max_exact) / math.log(MAX_DISTANCE / max_exact)
                         * (N_BUCKETS - max_exact)).astype(np.int32)
    large = np.minimum(large, N_BUCKETS - 1)
    return np.where(dist < max_exact, dist, large).astype(np.int32)


def _mlstm(q, k, v, ig, fg, C0, n0, m0):
    f32 = jnp.float32
    B, S, H, DK = q.shape
    L = MLSTM_CHUNK if S % MLSTM_CHUNK == 0 else S
    nc = S // L

    def chunks(t):
        t = t.astype(f32).reshape((B, nc, L, H) + t.shape[3:])
        return jnp.moveaxis(jnp.moveaxis(t, 1, 0), 3, 2)

    qc = chunks(q)
    kc = chunks(k.astype(f32) * (DK ** -0.5))
    vc = chunks(v)
    igc = chunks(ig)
    lfc = chunks(jax.nn.log_sigmoid(fg.astype(f32)))
    causal = np.tril(np.ones((L, L), dtype=bool))

    def step(carry, xs):
        C, n, m = carry
        qx, kx, vx, ix, lf = xs
        b = jnp.cumsum(lf, axis=-1)
        dmat = jnp.where(causal, b[..., :, None] - b[..., None, :] + ix[..., None, :], -jnp.inf)
        inter = b + m[..., None]
        mt = jnp.maximum(inter, dmat.max(-1))
        w = jnp.einsum('bhld,bhsd->bhls', qx, kx) * jnp.exp(dmat - mt[..., None])
        a = jnp.exp(inter - mt)
        num = jnp.einsum('bhls,bhse->bhle', w, vx) + a[..., None] * jnp.einsum('bhld,bhde->bhle', qx, C)
        den = w.sum(-1) + a * jnp.einsum('bhld,bhd->bhl', qx, n)
        h = num / jnp.maximum(jnp.abs(den), jnp.exp(-mt))[..., None]
        m_new = mt[..., -1]
        g = jnp.exp(b[..., -1:] - b + ix - m_new[..., None])
        dec = jnp.exp(b[..., -1] + m - m_new)
        C_new = dec[..., None, None] * C + jnp.einsum('bhs,bhsd,bhse->bhde', g, kx, vx)
        n_new = dec[..., None] * n + jnp.einsum('bhs,bhsd->bhd', g, kx)
        return (C_new, n_new, m_new), h

    (C, n, m), hs = lax.scan(step, (C0.astype(f32), n0.astype(f32), m0.astype(f32)),
                             (qc, kc, vc, igc, lfc))
    h = jnp.swapaxes(jnp.moveaxis(hs, 0, 1), 2, 3).reshape(B, S, H, -1)
    return h, (C, n, m)


def _merge(outs, lses):
    wts = jax.nn.softmax(jnp.stack(lses, 0), axis=0)
    return jnp.einsum('pbsh,pbshd->bshd', wts, jnp.stack(outs, 0).astype(jnp.float32))


def _dilated_prompt(q, k, v, rel_bias):
    B, S, H, D = q.shape
    scale = D ** -0.5
    outs, lses = [], []
    for (w, r) in DILATED_PATTERNS:
        nw = w // r
        L = S // r
        nb = -(-L // nw)
        Lp = nb * nw

        def to_res(t):
            return t.reshape(B, L, r, H, D).transpose(0, 2, 1, 3, 4)

        def key_blocks(t):
            tp = jnp.pad(t, ((0, 0), (0, 0), (nw, Lp - L), (0, 0), (0, 0))).reshape(B, r, nb + 1, nw, H, D)
            return jnp.concatenate([tp[:, :, :-1], tp[:, :, 1:]], axis=3)

        qb = jnp.pad(to_res(q), ((0, 0), (0, 0), (0, Lp - L), (0, 0), (0, 0))).reshape(B, r, nb, nw, H, D)
        kb = key_blocks(to_res(k))
        vb = key_blocks(to_res(v))
        qi = np.arange(nw)[:, None]
        ki = np.arange(2 * nw)[None, :]
        delta = qi + nw - ki
        bias_r = rel_bias[_t5_bucket(np.arange(nw + 1) * r)]
        bias = bias_r[np.clip(delta, 0, nw)].transpose(2, 0, 1).astype(jnp.float32)
        blk = np.arange(nb)[:, None, None]
        valid = (delta >= 0) & (delta <= nw) & (blk * nw + ki[None] - nw >= 0)
        s = jnp.einsum('brnqhd,brnkhd->brnhqk', qb, kb).astype(jnp.float32) * scale + bias
        s = jnp.where(valid[None, None, :, None], s, NEG_INF)
        lse = jax.nn.logsumexp(s, axis=-1)
        p = jnp.exp(s - lse[..., None])
        o = jnp.einsum('brnhqk,brnkhd->brnqhd', p, vb.astype(jnp.float32))
        o = o.reshape(B, r, Lp, H, D)[:, :, :L].transpose(0, 2, 1, 3, 4).reshape(B, S, H, D)
        lse = lse.transpose(0, 1, 2, 4, 3).reshape(B, r, Lp, H)[:, :, :L].transpose(0, 2, 1, 3).reshape(B, S, H)
        outs.append(o)
        lses.append(lse)
    return _merge(outs, lses)


def _dilated_sample(q, kc, vc, rel_bias, wbuf):
    B, T, H, D = q.shape
    scale = D ** -0.5
    outs, lses = [], []
    for (w, r) in DILATED_PATTERNS:
        nw = w // r
        j = np.arange(nw + 1)
        idx = wbuf + np.arange(T)[:, None] - j[None, :] * r
        valid = idx >= 0
        idxc = np.maximum(idx, 0)
        kg = kc[:, idxc]
        vg = vc[:, idxc]
        bias_r = rel_bias[_t5_bucket(j * r)].astype(jnp.float32)
        s = jnp.einsum('bthd,btjhd->bhtj', q, kg).astype(jnp.float32) * scale + bias_r.T[None, :, None, :]
        s = jnp.where(valid[None, None], s, NEG_INF)
        lse = jax.nn.logsumexp(s, axis=-1)
        p = jnp.exp(s - lse[..., None])
        outs.append(jnp.einsum('bhtj,btjhd->bthd', p, vg.astype(jnp.float32)))
        lses.append(lse.transpose(0, 2, 1))
    return _merge(outs, lses)


def _block(x, C0, n0, m0, kbuf, vbuf, rel_bias, g1, w1g, w1u, w1d, gm, w_in, b_i, b_f,
           g_head, w_out, g2, w2g, w2u, w2d):
    B, S, _ = x.shape
    x = x + 0.5 * _swiglu(_rmsnorm(x, g1), w1g, w1u, w1d)
    h = _rmsnorm(x, gm)
    qa, ka, va, oa, ia, fa, qb, kb, vb = jnp.split(h @ w_in, SPLIT_POINTS, axis=-1)
    hA, (C, n, m) = _mlstm(qa.reshape(B, S, N_HEADS_A, DK_A), ka.reshape(B, S, N_HEADS_A, DK_A),
                           va.reshape(B, S, N_HEADS_A, DV_A), ia + b_i, fa + b_f, C0, n0, m0)
    hA = hA * lax.rsqrt(jnp.mean(hA * hA, axis=-1, keepdims=True) + EPS)
    yA = (hA * g_head.reshape(N_HEADS_A, DV_A).astype(jnp.float32)
          * jax.nn.sigmoid(oa.reshape(B, S, N_HEADS_A, DV_A).astype(jnp.float32)))
    yA = yA.reshape(B, S, MIX_A).astype(x.dtype)
    qb = qb.reshape(B, S, N_HEADS_B, HEAD_DIM)
    kb = kb.reshape(B, S, N_HEADS_B, HEAD_DIM)
    vb = vb.reshape(B, S, N_HEADS_B, HEAD_DIM)
    if kbuf is None:
        yB = _dilated_prompt(qb, kb, vb, rel_bias)
        wkeep = min(MAX_WINDOW, S)
        new_k, new_v = kb[:, S - wkeep:], vb[:, S - wkeep:]
    else:
        wbuf = kbuf.shape[1]
        kc = jnp.concatenate([kbuf.astype(kb.dtype), kb], axis=1)
        vc = jnp.concatenate([vbuf.astype(vb.dtype), vb], axis=1)
        yB = _dilated_sample(qb, kc, vc, rel_bias, wbuf)
        new_k, new_v = kc[:, S:], vc[:, S:]
    y = jnp.concatenate([yA, yB.reshape(B, S, MIX_B).astype(x.dtype)], axis=-1) @ w_out
    x = x + y
    x = x + 0.5 * _swiglu(_rmsnorm(x, g2), w2g, w2u, w2d)
    return x, (C, n, m, new_k, new_v)


def setup_inputs(seed: int = 0) -> dict:
    key = jax.random.key(seed)
    ks = jax.random.split(key, 24)
    f32 = jnp.float32

    def nrm(i, shape, scale):
        return jax.random.normal(ks[i], shape, f32) * scale

    wbuf = min(MAX_WINDOW, PAST_LEN)
    return {
        "x_prompt": nrm(0, (BATCH, SEQ, D_MODEL), 1.0),
        "x_sample": nrm(1, (DEC_BATCH, DEC_SEQ, D_MODEL), 1.0),
        "state_mlstm_C": nrm(2, (DEPTH, DEC_BATCH, N_HEADS_A, DK_A, DV_A), 0.05),
        "state_mlstm_n": nrm(3, (DEPTH, DEC_BATCH, N_HEADS_A, DK_A), 0.1),
        "state_mlstm_m": nrm(4, (DEPTH, DEC_BATCH, N_HEADS_A), 1.0),
        "cache_win_k": nrm(5, (DEPTH, DEC_BATCH, wbuf, N_HEADS_B, HEAD_DIM), 1.0),
        "cache_win_v": nrm(6, (DEPTH, DEC_BATCH, wbuf, N_HEADS_B, HEAD_DIM), 1.0),
        "g_ffn1": 1.0 + nrm(7, (DEPTH, D_MODEL), 0.02),
        "w_ffn1_gate": nrm(8, (DEPTH, D_MODEL, D_FF), D_MODEL ** -0.5),
        "w_ffn1_up": nrm(9, (DEPTH, D_MODEL, D_FF), D_MODEL ** -0.5),
        "w_ffn1_down": nrm(10, (DEPTH, D_FF, D_MODEL), D_FF ** -0.5),
        "g_mix": 1.0 + nrm(11, (DEPTH, D_MODEL), 0.02),
        "w_in": nrm(12, (DEPTH, D_MODEL, N_IN), D_MODEL ** -0.5),
        "b_igate": nrm(13, (DEPTH, N_HEADS_A), 0.1),
        "b_fgate": 3.0 + nrm(14, (DEPTH, N_HEADS_A), 0.5),
        "g_head": 1.0 + nrm(15, (DEPTH, MIX_A), 0.02),
        "w_out": nrm(16, (DEPTH, MIX_WIDTH, D_MODEL), MIX_WIDTH ** -0.5),
        "rel_bias": nrm(17, (N_BUCKETS, N_HEADS_B), 0.5),
        "g_ffn2": 1.0 + nrm(18, (DEPTH, D_MODEL), 0.02),
        "w_ffn2_gate": nrm(19, (DEPTH, D_MODEL, D_FF), D_MODEL ** -0.5),
        "w_ffn2_up": nrm(20, (DEPTH, D_MODEL, D_FF), D_MODEL ** -0.5),
        "w_ffn2_down": nrm(21, (DEPTH, D_FF, D_MODEL), D_FF ** -0.5),
        "g_final": 1.0 + nrm(22, (D_MODEL,), 0.02),
    }


def reference(x_prompt, x_sample, state_mlstm_C, state_mlstm_n, state_mlstm_m, cache_win_k, cache_win_v,
              g_ffn1, w_ffn1_gate, w_ffn1_up, w_ffn1_down, g_mix, w_in, b_igate, b_fgate, g_head, w_out,
              rel_bias, g_ffn2, w_ffn2_gate, w_ffn2_up, w_ffn2_down, g_final):
    bp = x_prompt.shape[0]
    zC = jnp.zeros((bp, N_HEADS_A, DK_A, DV_A), jnp.float32)
    zn = jnp.zeros((bp, N_HEADS_A, DK_A), jnp.float32)
    zm = jnp.zeros((bp, N_HEADS_A), jnp.float32)
    yp, ys = x_prompt, x_sample
    pC, pn, pm, pk, pv = [], [], [], [], []
    sC, sn, sm, sk, sv = [], [], [], [], []
    for l in range(DEPTH):
        lw = (rel_bias, g_ffn1[l], w_ffn1_gate[l], w_ffn1_up[l], w_ffn1_down[l], g_mix[l], w_in[l],
              b_igate[l], b_fgate[l], g_head[l], w_out[l], g_ffn2[l], w_ffn2_gate[l], w_ffn2_up[l],
              w_ffn2_down[l])
        yp, (c_, n_, m_, k_, v_) = _block(yp, zC, zn, zm, None, None, *lw)
        pC.append(c_); pn.append(n_); pm.append(m_); pk.append(k_); pv.append(v_)
        ys, (c_, n_, m_, k_, v_) = _block(ys, state_mlstm_C[l], state_mlstm_n[l], state_mlstm_m[l],
                                          cache_win_k[l], cache_win_v[l], *lw)
        sC.append(c_); sn.append(n_); sm.append(m_); sk.append(k_); sv.append(v_)
    yp = _rmsnorm(yp, g_final)
    ys = _rmsnorm(ys, g_final)
    return (yp, ys, jnp.stack(pC), jnp.stack(pn), jnp.stack(pm), jnp.stack(pk), jnp.stack(pv),
            jnp.stack(sC), jnp.stack(sn), jnp.stack(sm), jnp.stack(sk), jnp.stack(sv))
```

```python
import functools
import math

import numpy as np
import jax
import jax.numpy as jnp
from jax import lax
from jax.experimental import pallas as pl
from jax.experimental.pallas import tpu as pltpu

F32 = jnp.float32
BF16 = jnp.bfloat16

D_MODEL = 4096
N_HEADS_A = 8
DK_A = 128
DV_A = 256
MIX_A = N_HEADS_A * DV_A
N_HEADS_B = 16
HEAD_DIM = 128
MIX_B = N_HEADS_B * HEAD_DIM
PATTERNS = ((128, 1), (512, 4), (2048, 16))
N_STEPS = 128
N_BUCKETS = 32
MAX_DISTANCE = 2048
D_FF = 11008
EPS = 1e-6
NEG_INF = -1e30
A_WIDTH = 2 * N_HEADS_A * DK_A + 2 * MIX_A
GATE_COLS = 2 * N_HEADS_A
B_WIDTH = 3 * MIX_B
LANES = 128
PROMPT_CHUNK = 256

VMEM_LIMIT = 48 * 1024 * 1024


def _params(*sem):
    return pltpu.CompilerParams(dimension_semantics=sem, vmem_limit_bytes=VMEM_LIMIT)


def _rmsnorm_kernel(x_ref, g_ref, o_ref):
    x = x_ref[...]
    ms = jnp.mean(x * x, axis=-1, keepdims=True)
    o_ref[...] = (x * lax.rsqrt(ms + EPS) * g_ref[...]).astype(o_ref.dtype)


def _rmsnorm(x, g, out_dtype):
    m, d = x.shape
    tm = min(m, 256)
    return pl.pallas_call(
        _rmsnorm_kernel,
        grid=(m // tm,),
        in_specs=[pl.BlockSpec((tm, d), lambda i: (i, 0)),
                  pl.BlockSpec((1, d), lambda i: (0, 0))],
        out_specs=pl.BlockSpec((tm, d), lambda i: (i, 0)),
        out_shape=jax.ShapeDtypeStruct((m, d), out_dtype),
        compiler_params=_params("parallel"),
        name="rmsnorm",
    )(x, g.reshape(1, d))


def _matmul_kernel(x_ref, w_ref, o_ref):
    o_ref[...] = jnp.dot(x_ref[...], w_ref[...], preferred_element_type=F32).astype(o_ref.dtype)


def _matmul(x, w, tn, out_dtype=F32):
    m, k = x.shape
    n = w.shape[1]
    tm = min(m, 1024)
    return pl.pallas_call(
        _matmul_kernel,
        grid=(m // tm, n // tn),
        in_specs=[pl.BlockSpec((tm, k), lambda i, j: (i, 0)),
                  pl.BlockSpec((k, tn), lambda i, j: (0, j))],
        out_specs=pl.BlockSpec((tm, tn), lambda i, j: (i, j)),
        out_shape=jax.ShapeDtypeStruct((m, n), out_dtype),
        compiler_params=_params("parallel", "arbitrary"),
        name="matmul",
    )(x, w)


def _ffn_up_kernel(x_ref, wg_ref, wu_ref, o_ref):
    x = x_ref[...]
    g = jnp.dot(x, wg_ref[...], preferred_element_type=F32)
    u = jnp.dot(x, wu_ref[...], preferred_element_type=F32)
    o_ref[...] = (g * jax.nn.sigmoid(g) * u).astype(o_ref.dtype)


def _ffn_up(x, wg, wu):
    m, k = x.shape
    n = wg.shape[1]
    tm = min(m, 1024)
    tn = 256
    return pl.pallas_call(
        _ffn_up_kernel,
        grid=(m // tm, n // tn),
        in_specs=[pl.BlockSpec((tm, k), lambda i, j: (i, 0)),
                  pl.BlockSpec((k, tn), lambda i, j: (0, j)),
                  pl.BlockSpec((k, tn), lambda i, j: (0, j))],
        out_specs=pl.BlockSpec((tm, tn), lambda i, j: (i, j)),
        out_shape=jax.ShapeDtypeStruct((m, n), BF16),
        compiler_params=_params("parallel", "arbitrary"),
        name="ffn_up",
    )(x, wg, wu)


def _matmul_res_kernel(a_ref, w_ref, r_ref, o_ref, *, scale):
    acc = jnp.dot(a_ref[...], w_ref[...], preferred_element_type=F32)
    o_ref[...] = r_ref[...] + scale * acc


def _matmul_res(a, w, res, scale, tm, tn):
    m, k = a.shape
    n = w.shape[1]
    tm = min(m, tm)
    return pl.pallas_call(
        functools.partial(_matmul_res_kernel, scale=scale),
        grid=(m // tm, n // tn),
        in_specs=[pl.BlockSpec((tm, k), lambda i, j: (i, 0)),
                  pl.BlockSpec((k, tn), lambda i, j: (0, j)),
                  pl.BlockSpec((tm, tn), lambda i, j: (i, j))],
        out_specs=pl.BlockSpec((tm, tn), lambda i, j: (i, j)),
        out_shape=jax.ShapeDtypeStruct((m, n), F32),
        compiler_params=_params("parallel", "arbitrary"),
        name="matmul_res",
    )(a, w, res)


def _mlstm_kernel(q_ref, k_ref, v_ref, o_ref, g_ref, gb_ref, gh_ref, c0_ref, n0_ref, m0_ref,
                  y_ref, c_out_ref, n_out_ref, m_out_ref, c_scr, n_scr, m_scr, *, chunk, n_chunks):
    head = pl.program_id(1)
    c_idx = pl.program_id(2)
    L = chunk

    @pl.when(c_idx == 0)
    def _():
        c_scr[...] = c0_ref[0, 0]
        n_scr[...] = n0_ref[0]
        m_scr[...] = m0_ref[0]

    gates = g_ref[...] + gb_ref[...]
    lane = lax.broadcasted_iota(jnp.int32, gates.shape, 1)
    i_col = jnp.sum(jnp.where(lane == head, gates, 0.0), axis=1, keepdims=True)
    f_col = jnp.sum(jnp.where(lane == head + N_HEADS_A, gates, 0.0), axis=1, keepdims=True)
    lf_col = jax.nn.log_sigmoid(f_col)

    row = lax.broadcasted_iota(jnp.int32, (L, L), 0)
    col = lax.broadcasted_iota(jnp.int32, (L, L), 1)
    eye = row == col
    causal = col <= row
    lf_row = jnp.sum(jnp.where(eye, lf_col, 0.0), axis=0, keepdims=True)
    i_row = jnp.sum(jnp.where(eye, i_col, 0.0), axis=0, keepdims=True)
    b_col = jnp.sum(jnp.where(causal, lf_row, 0.0), axis=1, keepdims=True)
    b_row = jnp.sum(jnp.where(row <= col, lf_col, 0.0), axis=0, keepdims=True)

    m_prev = m_scr[...]
    dmat = jnp.where(causal, b_col - b_row + i_row, -jnp.inf)
    inter = b_col + m_prev
    mt = jnp.maximum(inter, jnp.max(dmat, axis=1, keepdims=True))

    q32 = q_ref[...]
    k32 = k_ref[...] * (DK_A ** -0.5)
    v32 = v_ref[...]
    q = q32.astype(BF16)
    k = k32.astype(BF16)
    v = v32.astype(BF16)
    c_prev = c_scr[...]
    n_prev = n_scr[...]

    s = lax.dot_general(q, k, (((1,), (1,)), ((), ())), preferred_element_type=F32)
    w = s * jnp.exp(dmat - mt)
    a = jnp.exp(inter - mt)
    num = (jnp.dot(w.astype(BF16), v, preferred_element_type=F32)
           + a * jnp.dot(q, c_prev.astype(BF16), preferred_element_type=F32))
    den = (jnp.sum(w, axis=1, keepdims=True)
           + a * jnp.sum(q32 * n_prev, axis=1, keepdims=True))
    h = num / jnp.maximum(jnp.abs(den), jnp.exp(-mt))

    m_new = mt[L - 1:L, :]
    b_last = b_col[L - 1:L, :]
    g = jnp.exp(b_last - b_col + i_col - m_new)
    dec = jnp.exp(b_last + m_prev - m_new)
    gk = g * k32
    c_new = dec * c_prev + lax.dot_general(gk.astype(BF16), v, (((0,), (0,)), ((), ())),
                                           preferred_element_type=F32)
    n_new = dec * n_prev + jnp.sum(gk, axis=0, keepdims=True)
    c_scr[...] = c_new
    n_scr[...] = n_new
    m_scr[...] = m_new

    hn = h * lax.rsqrt(jnp.mean(h * h, axis=-1, keepdims=True) + EPS)
    y_ref[...] = (hn * gh_ref[...] * jax.nn.sigmoid(o_ref[...])).astype(y_ref.dtype)

    @pl.when(c_idx == n_chunks - 1)
    def _():
        c_out_ref[0, 0] = c_new
        n_out_ref[0] = n_new
        m_out_ref[0] = m_new


def _mlstm(pa, gates, gate_bias, g_head, c0, n0, m0, batch, seq, chunk, out_dtype):
    n_chunks = seq // chunk
    m_rows = batch * seq
    k_off = (N_HEADS_A * DK_A) // DK_A
    v_off = (2 * N_HEADS_A * DK_A) // DV_A
    o_off = (2 * N_HEADS_A * DK_A + MIX_A) // DV_A
    bh = batch * N_HEADS_A

    def rows(b, h, c):
        return b * n_chunks + c

    kernel = functools.partial(_mlstm_kernel, chunk=chunk, n_chunks=n_chunks)
    y, c_fin, n_fin, m_fin = pl.pallas_call(
        kernel,
        grid=(batch, N_HEADS_A, n_chunks),
        in_specs=[
            pl.BlockSpec((chunk, DK_A), lambda b, h, c: (rows(b, h, c), h)),
            pl.BlockSpec((chunk, DK_A), lambda b, h, c: (rows(b, h, c), k_off + h)),
            pl.BlockSpec((chunk, DV_A), lambda b, h, c: (rows(b, h, c), v_off + h)),
            pl.BlockSpec((chunk, DV_A), lambda b, h, c: (rows(b, h, c), o_off + h)),
            pl.BlockSpec((chunk, LANES), lambda b, h, c: (rows(b, h, c), 0)),
            pl.BlockSpec((1, LANES), lambda b, h, c: (0, 0)),
            pl.BlockSpec((1, DV_A), lambda b, h, c: (0, h)),
            pl.BlockSpec((1, 1, DK_A, DV_A), lambda b, h, c: (b, h, 0, 0)),
            pl.BlockSpec((1, 1, DK_A), lambda b, h, c: (b * N_HEADS_A + h, 0, 0)),
            pl.BlockSpec((1, 1, 1), lambda b, h, c: (b * N_HEADS_A + h, 0, 0)),
        ],
        out_specs=[
            pl.BlockSpec((chunk, DV_A), lambda b, h, c: (rows(b, h, c), h)),
            pl.BlockSpec((1, 1, DK_A, DV_A), lambda b, h, c: (b, h, 0, 0)),
            pl.BlockSpec((1, 1, DK_A), lambda b, h, c: (b * N_HEADS_A + h, 0, 0)),
            pl.BlockSpec((1, 1, 1), lambda b, h, c: (b * N_HEADS_A + h, 0, 0)),
        ],
        out_shape=[
            jax.ShapeDtypeStruct((m_rows, MIX_A), out_dtype),
            jax.ShapeDtypeStruct((batch, N_HEADS_A, DK_A, DV_A), F32),
            jax.ShapeDtypeStruct((bh, 1, DK_A), F32),
            jax.ShapeDtypeStruct((bh, 1, 1), F32),
        ],
        scratch_shapes=[pltpu.VMEM((DK_A, DV_A), F32),
                        pltpu.VMEM((1, DK_A), F32),
                        pltpu.VMEM((1, 1), F32)],
        compiler_params=_params("parallel", "parallel", "arbitrary"),
        name="mlstm",
    )(pa, pa, pa, pa, gates, gate_bias, g_head.reshape(1, MIX_A),
      c0, n0.reshape(bh, 1, DK_A), m0.reshape(bh, 1, 1))
    return (y, c_fin, n_fin.reshape(batch, N_HEADS_A, DK_A), m_fin.reshape(batch, N_HEADS_A))


def _t5_bucket(dist):
    max_exact = N_BUCKETS // 2
    d = np.maximum(dist, 1).astype(np.float64)
    large = max_exact + (np.log(d / max_exact) / math.log(MAX_DISTANCE / max_exact)
                         * (N_BUCKETS - max_exact)).astype(np.int32)
    large = np.minimum(large, N_BUCKETS - 1)
    return np.where(dist < max_exact, dist, large).astype(np.int32)


def _prompt_bias_tiles(rel_bias):
    qi = np.arange(N_STEPS)[:, None]
    ki = np.arange(2 * N_STEPS)[None, :]
    delta = qi + N_STEPS - ki
    valid = (delta >= 0) & (delta <= N_STEPS)
    tiles = []
    for (_, r) in PATTERNS:
        bucket = _t5_bucket(np.clip(delta, 0, N_STEPS) * r)
        b = rel_bias[bucket].astype(F32)
        tiles.append(jnp.where(valid[:, :, None], b, NEG_INF))
    return jnp.stack(tiles, 0).transpose(3, 0, 1, 2)


def _attn_prompt_kernel(q_ref, k_ref, v_ref, bias_ref, y_ref, acc_scr, m_scr, l_scr, *, seq):
    scale = HEAD_DIM ** -0.5
    nw = N_STEPS

    def block(p, r, start, with_prev):
        rows = pl.ds(start, nw, stride=r) if r > 1 else pl.ds(start, nw)
        q = q_ref[rows, :].astype(BF16)
        if with_prev:
            prev = pl.ds(start - nw * r, nw, stride=r) if r > 1 else pl.ds(start - nw * r, nw)
            kk = jnp.concatenate([k_ref[prev, :], k_ref[rows, :]], axis=0).astype(BF16)
            vv = jnp.concatenate([v_ref[prev, :], v_ref[rows, :]], axis=0).astype(BF16)
            bias = bias_ref[0, p]
        else:
            kk = k_ref[rows, :].astype(BF16)
            vv = v_ref[rows, :].astype(BF16)
            bias = bias_ref[0, p, :, nw:]
        s = lax.dot_general(q, kk, (((1,), (1,)), ((), ())), preferred_element_type=F32)
        s = s * scale + bias
        m = jnp.max(s, axis=1, keepdims=True)
        e = jnp.exp(s - m)
        l = jnp.sum(e, axis=1, keepdims=True)
        acc = jnp.dot(e.astype(BF16), vv, preferred_element_type=F32)
        acc_scr[p, rows, :] = acc
        m_scr[p, rows, :] = jnp.broadcast_to(m, (nw, HEAD_DIM))
        l_scr[p, rows, :] = jnp.broadcast_to(l, (nw, HEAD_DIM))

    for p, (_, r) in enumerate(PATTERNS):
        nb = seq // (r * nw)

        def residue(c, carry, p=p, r=r, nb=nb):
            block(p, r, c, False)
            if nb > 1:
                def inner(n, carry2):
                    block(p, r, c + n * (nw * r), True)
                    return carry2
                lax.fori_loop(1, nb, inner, 0)
            return carry

        if r == 1:
            residue(0, 0)
        else:
            lax.fori_loop(0, r, residue, 0)

    rows_per_step = 256
    def merge(t, carry):
        rows = pl.ds(pl.multiple_of(t * rows_per_step, rows_per_step), rows_per_step)
        m0, m1, m2 = m_scr[0, rows, :], m_scr[1, rows, :], m_scr[2, rows, :]
        mx = jnp.maximum(jnp.maximum(m0, m1), m2)
        w0, w1, w2 = jnp.exp(m0 - mx), jnp.exp(m1 - mx), jnp.exp(m2 - mx)
        num = w0 * acc_scr[0, rows, :] + w1 * acc_scr[1, rows, :] + w2 * acc_scr[2, rows, :]
        den = w0 * l_scr[0, rows, :] + w1 * l_scr[1, rows, :] + w2 * l_scr[2, rows, :]
        y_ref[rows, :] = (num / den).astype(y_ref.dtype)
        return carry
    lax.fori_loop(0, seq // rows_per_step, merge, 0)


def _attn_prompt(pb, bias_tiles, batch, seq):
    n_pat = len(PATTERNS)
    kernel = functools.partial(_attn_prompt_kernel, seq=seq)
    return pl.pallas_call(
        kernel,
        grid=(batch, N_HEADS_B),
        in_specs=[
            pl.BlockSpec((seq, HEAD_DIM), lambda b, h: (b, h)),
            pl.BlockSpec((seq, HEAD_DIM), lambda b, h: (b, N_HEADS_B + h)),
            pl.BlockSpec((seq, HEAD_DIM), lambda b, h: (b, 2 * N_HEADS_B + h)),
            pl.BlockSpec((1, n_pat, N_STEPS, 2 * N_STEPS), lambda b, h: (h, 0, 0, 0)),
        ],
        out_specs=pl.BlockSpec((seq, HEAD_DIM), lambda b, h: (b, h)),
        out_shape=jax.ShapeDtypeStruct((batch * seq, MIX_B), BF16),
        scratch_shapes=[pltpu.VMEM((n_pat, seq, HEAD_DIM), F32),
                        pltpu.VMEM((n_pat, seq, HEAD_DIM), F32),
                        pltpu.VMEM((n_pat, seq, HEAD_DIM), F32)],
        compiler_params=_params("parallel", "parallel"),
        name="attn_prompt",
    )(pb, pb, pb, bias_tiles)


def _sample_bias(rel_bias, wbuf, t_new):
    t = np.arange(t_new)[:, None]
    out_c, out_n = [], []
    for (w, r) in PATTERNS:
        nwin = w // r
        for (cols, out) in ((np.arange(wbuf)[None, :] - wbuf, out_c), (np.arange(t_new)[None, :], out_n)):
            delta = t - cols
            valid = (delta >= 0) & (delta % r == 0) & (delta // r <= nwin)
            bucket = _t5_bucket(np.clip(delta, 0, w))
            b = rel_bias[bucket].astype(F32)
            out.append(jnp.where(valid[:, :, None], b, NEG_INF))
    bias_c = jnp.stack(out_c, 0).transpose(3, 0, 1, 2)
    bias_n = jnp.stack(out_n, 0).transpose(3, 0, 1, 2)
    return bias_c, bias_n


def _attn_sample_kernel(q_ref, kn_ref, vn_ref, kc_ref, vc_ref, bc_ref, bn_ref, y_ref):
    scale = HEAD_DIM ** -0.5
    q = q_ref[...].astype(BF16)
    kc = kc_ref[0].astype(BF16)
    vc = vc_ref[0].astype(BF16)
    kn = kn_ref[...].astype(BF16)
    vn = vn_ref[...].astype(BF16)
    s_c = lax.dot_general(q, kc, (((1,), (1,)), ((), ())), preferred_element_type=F32) * scale
    s_n = lax.dot_general(q, kn, (((1,), (1,)), ((), ())), preferred_element_type=F32) * scale
    n_pat = len(PATTERNS)
    zc = [s_c + bc_ref[0, p] for p in range(n_pat)]
    zn = [s_n + bn_ref[0, p] for p in range(n_pat)]
    mx = None
    for z in zc + zn:
        zm = jnp.max(z, axis=1, keepdims=True)
        mx = zm if mx is None else jnp.maximum(mx, zm)
    pc = sum(jnp.exp(z - mx) for z in zc)
    pn = sum(jnp.exp(z - mx) for z in zn)
    den = jnp.sum(pc, axis=1, keepdims=True) + jnp.sum(pn, axis=1, keepdims=True)
    num = (jnp.dot(pc.astype(BF16), vc, preferred_element_type=F32)
           + jnp.dot(pn.astype(BF16), vn, preferred_element_type=F32))
    y_ref[...] = (num / den).astype(y_ref.dtype)


def _attn_sample(pb, cache_k, cache_v, bias_c, bias_n, batch, t_new):
    wbuf = cache_k.shape[1]
    n_pat = len(PATTERNS)
    return pl.pallas_call(
        _attn_sample_kernel,
        grid=(batch, N_HEADS_B),
        in_specs=[
            pl.BlockSpec((t_new, HEAD_DIM), lambda b, h: (b, h)),
            pl.BlockSpec((t_new, HEAD_DIM), lambda b, h: (b, N_HEADS_B + h)),
            pl.BlockSpec((t_new, HEAD_DIM), lambda b, h: (b, 2 * N_HEADS_B + h)),
            pl.BlockSpec((1, wbuf, HEAD_DIM), lambda b, h: (b, 0, h)),
            pl.BlockSpec((1, wbuf, HEAD_DIM), lambda b, h: (b, 0, h)),
            pl.BlockSpec((1, n_pat, t_new, wbuf), lambda b, h: (h, 0, 0, 0)),
            pl.BlockSpec((1, n_pat, t_new, t_new), lambda b, h: (h, 0, 0, 0)),
        ],
        out_specs=pl.BlockSpec((t_new, HEAD_DIM), lambda b, h: (b, h)),
        out_shape=jax.ShapeDtypeStruct((batch * t_new, MIX_B), F32),
        compiler_params=_params("parallel", "parallel"),
        name="attn_sample",
    )(pb, pb, pb, cache_k, cache_v, bias_c, bias_n)


def _split_w_in(w_in_l):
    wa = w_in_l[:, :A_WIDTH].astype(BF16)
    wg = w_in_l[:, A_WIDTH:A_WIDTH + GATE_COLS]
    wg = jnp.pad(wg, ((0, 0), (0, LANES - GATE_COLS))).astype(BF16)
    wb = w_in_l[:, A_WIDTH + GATE_COLS:].astype(BF16)
    return wa, wg, wb


def _ffn(x, g, wg, wu, wd):
    h = _rmsnorm(x, g, BF16)
    a = _ffn_up(h, wg, wu)
    return _matmul_res(a, wd, x, 0.5, tm=512, tn=256)


def _mix_inputs(x, g_mix, wa, wgates, wb):
    h = _rmsnorm(x, g_mix, BF16)
    pa = _matmul(h, wa, 512)
    gates = _matmul(h, wgates, LANES)
    pb = _matmul(h, wb, 512)
    return pa, gates, pb


def kernel(x_prompt, x_sample, state_mlstm_C, state_mlstm_n, state_mlstm_m, cache_win_k, cache_win_v,
           g_ffn1, w_ffn1_gate, w_ffn1_up, w_ffn1_down, g_mix, w_in, b_igate, b_fgate, g_head, w_out,
           rel_bias, g_ffn2, w_ffn2_gate, w_ffn2_up, w_ffn2_down, g_final):
    bp, sp, d = x_prompt.shape
    bs, ts, _ = x_sample.shape
    depth = g_mix.shape[0]
    wbuf = cache_win_k.shape[2]

    xp = x_prompt.reshape(bp * sp, d)
    xs = x_sample.reshape(bs * ts, d)
    zc = jnp.zeros((bp, N_HEADS_A, DK_A, DV_A), F32)
    zn = jnp.zeros((bp, N_HEADS_A, DK_A), F32)
    zm = jnp.zeros((bp, N_HEADS_A), F32)
    bias_tiles = _prompt_bias_tiles(rel_bias)
    bias_c, bias_n = _sample_bias(rel_bias, wbuf, ts)

    p_c, p_n, p_m, p_k, p_v = [], [], [], [], []
    s_c, s_n, s_m, s_k, s_v = [], [], [], [], []
    for l in range(depth):
        w1g, w1u, w1d = (w_ffn1_gate[l].astype(BF16), w_ffn1_up[l].astype(BF16),
                         w_ffn1_down[l].astype(BF16))
        w2g, w2u, w2d = (w_ffn2_gate[l].astype(BF16), w_ffn2_up[l].astype(BF16),
                         w_ffn2_down[l].astype(BF16))
        wa, wgates, wb = _split_w_in(w_in[l])
        wo = w_out[l].astype(BF16)
        gate_bias = jnp.pad(jnp.concatenate([b_igate[l], b_fgate[l]]),
                            (0, LANES - GATE_COLS)).reshape(1, LANES)

        xp = _ffn(xp, g_ffn1[l], w1g, w1u, w1d)
        pa, gates, pb = _mix_inputs(xp, g_mix[l], wa, wgates, wb)
        ya, c_, n_, m_ = _mlstm(pa, gates, gate_bias, g_head[l], zc, zn, zm,
                                bp, sp, PROMPT_CHUNK, BF16)
        yb = _attn_prompt(pb, bias_tiles, bp, sp)
        xp = _matmul_res(jnp.concatenate([ya, yb], axis=1), wo, xp, 1.0, tm=1024, tn=512)
        xp = _ffn(xp, g_ffn2[l], w2g, w2u, w2d)
        p_c.append(c_); p_n.append(n_); p_m.append(m_)
        p_k.append(pb[:, MIX_B:2 * MIX_B].reshape(bp, sp, N_HEADS_B, HEAD_DIM))
        p_v.append(pb[:, 2 * MIX_B:].reshape(bp, sp, N_HEADS_B, HEAD_DIM))

        xs = _ffn(xs, g_ffn1[l], w1g, w1u, w1d)
        pa, gates, pb = _mix_inputs(xs, g_mix[l], wa, wgates, wb)
        ya, c_, n_, m_ = _mlstm(pa, gates, gate_bias, g_head[l], state_mlstm_C[l], state_mlstm_n[l],
                                state_mlstm_m[l], bs, ts, ts, F32)
        ck = cache_win_k[l].reshape(bs, wbuf, MIX_B)
        cv = cache_win_v[l].reshape(bs, wbuf, MIX_B)
        yb = _attn_sample(pb, ck, cv, bias_c, bias_n, bs, ts)
        y = jnp.concatenate([ya, yb], axis=1).astype(BF16)
        xs = _matmul_res(y, wo, xs, 1.0, tm=1024, tn=512)
        xs = _ffn(xs, g_ffn2[l], w2g, w2u, w2d)
        s_c.append(c_); s_n.append(n_); s_m.append(m_)
        kb_new = pb[:, MIX_B:2 * MIX_B].reshape(bs, ts, MIX_B)
        vb_new = pb[:, 2 * MIX_B:].reshape(bs, ts, MIX_B)
        s_k.append(jnp.concatenate([ck[:, ts:], kb_new], axis=1).reshape(bs, wbuf, N_HEADS_B, HEAD_DIM))
        s_v.append(jnp.concatenate([cv[:, ts:], vb_new], axis=1).reshape(bs, wbuf, N_HEADS_B, HEAD_DIM))

    yp = _rmsnorm(xp, g_final, F32).reshape(bp, sp, d)
    ys = _rmsnorm(xs, g_final, F32).reshape(bs, ts, d)
    return (yp, ys, jnp.stack(p_c), jnp.stack(p_n), jnp.stack(p_m), jnp.stack(p_k), jnp.stack(p_v),
            jnp.stack(s_c), jnp.stack(s_n), jnp.stack(s_m), jnp.stack(s_k), jnp.stack(s_v))
```

```python
import functools
import math

import numpy as np
import jax
import jax.numpy as jnp
from jax import lax
from jax.experimental import pallas as pl
from jax.experimental.pallas import tpu as pltpu

F32 = jnp.float32
BF16 = jnp.bfloat16

D_MODEL = 4096
N_HEADS_A = 8
DK_A = 128
DV_A = 256
MIX_A = N_HEADS_A * DV_A
N_HEADS_B = 16
HEAD_DIM = 128
MIX_B = N_HEADS_B * HEAD_DIM
PATTERNS = ((128, 1), (512, 4), (2048, 16))
N_STEPS = 128
N_BUCKETS = 32
MAX_DISTANCE = 2048
EPS = 1e-6
NEG_INF = -1e30
A_WIDTH = 2 * N_HEADS_A * DK_A + 2 * MIX_A
GATE_COLS = 2 * N_HEADS_A
LANES = 128
PROMPT_CHUNK = 256

VMEM_LIMIT = 56 * 1024 * 1024


def _params(*sem):
    return pltpu.CompilerParams(dimension_semantics=sem, vmem_limit_bytes=VMEM_LIMIT)


def _row_tile(m, candidates):
    for t in candidates:
        if m % t == 0:
            return t
    raise ValueError(f"no row tile for {m} rows among {candidates}")


def _rmsnorm_kernel(x_ref, g_ref, o_ref):
    x = x_ref[...]
    ms = jnp.mean(x * x, axis=-1, keepdims=True)
    o_ref[...] = (x * lax.rsqrt(ms + EPS) * g_ref[...]).astype(o_ref.dtype)


def _rmsnorm(x, g, out_dtype, row_start=0, n_rows=None):
    m, d = x.shape
    n_rows = m if n_rows is None else n_rows
    tm = _row_tile(math.gcd(n_rows, row_start) if row_start else n_rows, (256, 192, 128, 64))
    off = row_start // tm
    return pl.pallas_call(
        _rmsnorm_kernel,
        grid=(n_rows // tm,),
        in_specs=[pl.BlockSpec((tm, d), lambda i: (off + i, 0)),
                  pl.BlockSpec((1, d), lambda i: (0, 0))],
        out_specs=pl.BlockSpec((tm, d), lambda i: (i, 0)),
        out_shape=jax.ShapeDtypeStruct((n_rows, d), out_dtype),
        compiler_params=_params("parallel"),
        name="rmsnorm",
    )(x, g.reshape(1, d))


def _cast_kernel(w_ref, o_ref):
    o_ref[...] = w_ref[...].astype(o_ref.dtype)


def _cast_layer(w, layer, rows_per_step=256):
    _, k, n = w.shape
    return pl.pallas_call(
        _cast_kernel,
        grid=(k // rows_per_step,),
        in_specs=[pl.BlockSpec((None, rows_per_step, n), lambda i: (layer, i, 0))],
        out_specs=pl.BlockSpec((rows_per_step, n), lambda i: (i, 0)),
        out_shape=jax.ShapeDtypeStruct((k, n), BF16),
        compiler_params=_params("parallel"),
        name="cast_bf16",
    )(w)


def _matmul_w32_kernel(x_ref, w_ref, o_ref):
    w = w_ref[...].astype(BF16)
    o_ref[...] = jnp.dot(x_ref[...], w, preferred_element_type=F32).astype(o_ref.dtype)


def _matmul_w32(x, w, layer, col_block0, n_cols, tn, tm):
    m, k = x.shape
    return pl.pallas_call(
        _matmul_w32_kernel,
        grid=(m // tm, n_cols // tn),
        in_specs=[pl.BlockSpec((tm, k), lambda i, j: (i, 0)),
                  pl.BlockSpec((None, k, tn), lambda i, j: (layer, 0, col_block0 + j))],
        out_specs=pl.BlockSpec((tm, tn), lambda i, j: (i, j)),
        out_shape=jax.ShapeDtypeStruct((m, n_cols), F32),
        compiler_params=_params("parallel", "arbitrary"),
        name="matmul_w32",
    )(x, w)


def _ffn_up_kernel(x_ref, wg_ref, wu_ref, o_ref):
    x = x_ref[...]
    g = jnp.dot(x, wg_ref[...].astype(BF16), preferred_element_type=F32)
    u = jnp.dot(x, wu_ref[...].astype(BF16), preferred_element_type=F32)
    o_ref[...] = (g * jax.nn.sigmoid(g) * u).astype(o_ref.dtype)


def _ffn_up(x, wg, wu, layer, tm, tn=256):
    m, k = x.shape
    n = wg.shape[2]
    return pl.pallas_call(
        _ffn_up_kernel,
        grid=(m // tm, n // tn),
        in_specs=[pl.BlockSpec((tm, k), lambda i, j: (i, 0)),
                  pl.BlockSpec((None, k, tn), lambda i, j: (layer, 0, j)),
                  pl.BlockSpec((None, k, tn), lambda i, j: (layer, 0, j))],
        out_specs=pl.BlockSpec((tm, tn), lambda i, j: (i, j)),
        out_shape=jax.ShapeDtypeStruct((m, n), BF16),
        compiler_params=_params("parallel", "arbitrary"),
        name="ffn_up",
    )(x, wg, wu)


def _matmul_res_kernel(a_ref, w_ref, r_ref, o_ref, *, scale):
    acc = jnp.dot(a_ref[...], w_ref[...].astype(BF16), preferred_element_type=F32)
    o_ref[...] = r_ref[...] + scale * acc


def _matmul_res(a, w_spec, w, res, scale, tm, tn):
    m, _ = a.shape
    k = a.shape[1]
    n = res.shape[1]
    return pl.pallas_call(
        functools.partial(_matmul_res_kernel, scale=scale),
        grid=(m // tm, n // tn),
        in_specs=[pl.BlockSpec((tm, k), lambda i, j: (i, 0)),
                  w_spec,
                  pl.BlockSpec((tm, tn), lambda i, j: (i, j))],
        out_specs=pl.BlockSpec((tm, tn), lambda i, j: (i, j)),
        out_shape=jax.ShapeDtypeStruct((m, n), F32),
        compiler_params=_params("parallel", "arbitrary"),
        name="matmul_res",
    )(a, w, res)


def _qkv_kernel(x_ref, wq_ref, wk_ref, wv_ref, *refs):
    q_ref, k_ref, v_ref = refs[-3:]
    x = x_ref[...]
    q_ref[...] = jnp.dot(x, wq_ref[...], preferred_element_type=F32)
    k_ref[...] = jnp.dot(x, wk_ref[...], preferred_element_type=F32)
    v_ref[...] = jnp.dot(x, wv_ref[...], preferred_element_type=F32)


def _qkv_prompt(h, wb, layer, depth, n_rows, prev_k, prev_v, tm=1024, tn=256):
    k_dim = h.shape[1]
    nb = MIX_B // tn
    kv_shape = jax.ShapeDtypeStruct((depth, n_rows, MIX_B), F32)
    in_specs = [pl.BlockSpec((tm, k_dim), lambda i, j: (i, 0)),
                pl.BlockSpec((k_dim, tn), lambda i, j: (0, j)),
                pl.BlockSpec((k_dim, tn), lambda i, j: (0, nb + j)),
                pl.BlockSpec((k_dim, tn), lambda i, j: (0, 2 * nb + j))]
    args = [h, wb, wb, wb]
    aliases = {}
    if prev_k is not None:
        in_specs += [pl.BlockSpec(memory_space=pl.ANY), pl.BlockSpec(memory_space=pl.ANY)]
        args += [prev_k, prev_v]
        aliases = {4: 1, 5: 2}
    return pl.pallas_call(
        _qkv_kernel,
        grid=(n_rows // tm, nb),
        in_specs=in_specs,
        out_specs=[pl.BlockSpec((tm, tn), lambda i, j: (i, j)),
                   pl.BlockSpec((None, tm, tn), lambda i, j: (layer, i, j)),
                   pl.BlockSpec((None, tm, tn), lambda i, j: (layer, i, j))],
        out_shape=[jax.ShapeDtypeStruct((n_rows, MIX_B), F32), kv_shape, kv_shape],
        input_output_aliases=aliases,
        compiler_params=_params("parallel", "arbitrary"),
        name="qkv_prompt",
    )(*args)


def _qkv_sample(h, wb, row_start, n_rows, tn=512):
    k_dim = h.shape[1]
    nb = MIX_B // tn
    off = row_start // n_rows
    out = jax.ShapeDtypeStruct((n_rows, MIX_B), F32)
    return pl.pallas_call(
        _qkv_kernel,
        grid=(nb,),
        in_specs=[pl.BlockSpec((n_rows, k_dim), lambda j: (off, 0)),
                  pl.BlockSpec((k_dim, tn), lambda j: (0, j)),
                  pl.BlockSpec((k_dim, tn), lambda j: (0, nb + j)),
                  pl.BlockSpec((k_dim, tn), lambda j: (0, 2 * nb + j))],
        out_specs=[pl.BlockSpec((n_rows, tn), lambda j: (0, j))] * 3,
        out_shape=[out, out, out],
        compiler_params=_params("arbitrary"),
        name="qkv_sample",
    )(h, wb, wb, wb)


def _mlstm_kernel(q_ref, k_ref, v_ref, o_ref, g_ref, gb_ref, gh_ref, c0_ref, n0_ref, m0_ref,
                  y_ref, c_out_ref, n_out_ref, m_out_ref, c_scr, n_scr, m_scr, *, chunk, n_chunks):
    head = pl.program_id(1)
    c_idx = pl.program_id(2)
    L = chunk

    @pl.when(c_idx == 0)
    def _():
        c_scr[...] = c0_ref[0, 0]
        n_scr[...] = n0_ref[0]
        m_scr[...] = m0_ref[0]

    gates = g_ref[...] + gb_ref[...]
    lane = lax.broadcasted_iota(jnp.int32, gates.shape, 1)
    i_col = jnp.sum(jnp.where(lane == head, gates, 0.0), axis=1, keepdims=True)
    f_col = jnp.sum(jnp.where(lane == head + N_HEADS_A, gates, 0.0), axis=1, keepdims=True)
    lf_col = jax.nn.log_sigmoid(f_col)

    row = lax.broadcasted_iota(jnp.int32, (L, L), 0)
    col = lax.broadcasted_iota(jnp.int32, (L, L), 1)
    eye = row == col
    causal = col <= row
    lf_row = jnp.sum(jnp.where(eye, lf_col, 0.0), axis=0, keepdims=True)
    i_row = jnp.sum(jnp.where(eye, i_col, 0.0), axis=0, keepdims=True)
    b_col = jnp.sum(jnp.where(causal, lf_row, 0.0), axis=1, keepdims=True)
    b_row = jnp.sum(jnp.where(row <= col, lf_col, 0.0), axis=0, keepdims=True)

    m_prev = m_scr[...]
    dmat = jnp.where(causal, b_col - b_row + i_row, -jnp.inf)
    inter = b_col + m_prev
    mt = jnp.maximum(inter, jnp.max(dmat, axis=1, keepdims=True))

    q32 = q_ref[...]
    k32 = k_ref[...] * (DK_A ** -0.5)
    v32 = v_ref[...]
    q = q32.astype(BF16)
    k = k32.astype(BF16)
    v = v32.astype(BF16)
    c_prev = c_scr[...]
    n_prev = n_scr[...]

    s = lax.dot_general(q, k, (((1,), (1,)), ((), ())), preferred_element_type=F32)
    w = s * jnp.exp(dmat - mt)
    a = jnp.exp(inter - mt)
    num = (jnp.dot(w.astype(BF16), v, preferred_element_type=F32)
           + a * jnp.dot(q, c_prev.astype(BF16), preferred_element_type=F32))
    den = (jnp.sum(w, axis=1, keepdims=True)
           + a * jnp.sum(q32 * n_prev, axis=1, keepdims=True))
    h = num / jnp.maximum(jnp.abs(den), jnp.exp(-mt))

    m_new = mt[L - 1:L, :]
    b_last = b_col[L - 1:L, :]
    g = jnp.exp(b_last - b_col + i_col - m_new)
    dec = jnp.exp(b_last + m_prev - m_new)
    gk = g * k32
    c_new = dec * c_prev + lax.dot_general(gk.astype(BF16), v, (((0,), (0,)), ((), ())),
                                           preferred_element_type=F32)
    n_new = dec * n_prev + jnp.sum(gk, axis=0, keepdims=True)
    c_scr[...] = c_new
    n_scr[...] = n_new
    m_scr[...] = m_new

    hn = h * lax.rsqrt(jnp.mean(h * h, axis=-1, keepdims=True) + EPS)
    y_ref[...] = (hn * gh_ref[...] * jax.nn.sigmoid(o_ref[...])).astype(y_ref.dtype)

    @pl.when(c_idx == n_chunks - 1)
    def _():
        c_out_ref[0, 0] = c_new
        n_out_ref[0] = n_new
        m_out_ref[0] = m_new


def _mlstm(pa, gates, gate_bias, g_head, layer, c0, n0, m0, row_start, batch, seq, chunk, out_dtype):
    n_chunks = seq // chunk
    m_rows = batch * seq
    k_off = (N_HEADS_A * DK_A) // DK_A
    v_off = (2 * N_HEADS_A * DK_A) // DV_A
    o_off = (2 * N_HEADS_A * DK_A + MIX_A) // DV_A
    bh = batch * N_HEADS_A
    r0 = row_start // chunk

    def rows(b, h, c):
        return r0 + b * n_chunks + c

    kernel = functools.partial(_mlstm_kernel, chunk=chunk, n_chunks=n_chunks)
    y, c_fin, n_fin, m_fin = pl.pallas_call(
        kernel,
        grid=(batch, N_HEADS_A, n_chunks),
        in_specs=[
            pl.BlockSpec((chunk, DK_A), lambda b, h, c: (rows(b, h, c), h)),
            pl.BlockSpec((chunk, DK_A), lambda b, h, c: (rows(b, h, c), k_off + h)),
            pl.BlockSpec((chunk, DV_A), lambda b, h, c: (rows(b, h, c), v_off + h)),
            pl.BlockSpec((chunk, DV_A), lambda b, h, c: (rows(b, h, c), o_off + h)),
            pl.BlockSpec((chunk, LANES), lambda b, h, c: (rows(b, h, c), 0)),
            pl.BlockSpec((1, LANES), lambda b, h, c: (0, 0)),
            pl.BlockSpec((None, 1, DV_A), lambda b, h, c: (layer, 0, h)),
            pl.BlockSpec((1, 1, DK_A, DV_A), lambda b, h, c: (b, h, 0, 0)),
            pl.BlockSpec((1, 1, DK_A), lambda b, h, c: (b * N_HEADS_A + h, 0, 0)),
            pl.BlockSpec((1, 1, 1), lambda b, h, c: (b * N_HEADS_A + h, 0, 0)),
        ],
        out_specs=[
            pl.BlockSpec((chunk, DV_A), lambda b, h, c: (b * n_chunks + c, h)),
            pl.BlockSpec((1, 1, DK_A, DV_A), lambda b, h, c: (b, h, 0, 0)),
            pl.BlockSpec((1, 1, DK_A), lambda b, h, c: (b * N_HEADS_A + h, 0, 0)),
            pl.BlockSpec((1, 1, 1), lambda b, h, c: (b * N_HEADS_A + h, 0, 0)),
        ],
        out_shape=[
            jax.ShapeDtypeStruct((m_rows, MIX_A), out_dtype),
            jax.ShapeDtypeStruct((batch, N_HEADS_A, DK_A, DV_A), F32),
            jax.ShapeDtypeStruct((bh, 1, DK_A), F32),
            jax.ShapeDtypeStruct((bh, 1, 1), F32),
        ],
        scratch_shapes=[pltpu.VMEM((DK_A, DV_A), F32),
                        pltpu.VMEM((1, DK_A), F32),
                        pltpu.VMEM((1, 1), F32)],
        compiler_params=_params("parallel", "parallel", "arbitrary"),
        name="mlstm",
    )(pa, pa, pa, pa, gates, gate_bias, g_head.reshape(g_head.shape[0], 1, MIX_A),
      c0, n0.reshape(bh, 1, DK_A), m0.reshape(bh, 1, 1))
    return (y, c_fin, n_fin.reshape(batch, N_HEADS_A, DK_A), m_fin.reshape(batch, N_HEADS_A))


def _t5_bucket(dist):
    max_exact = N_BUCKETS // 2
    d = np.maximum(dist, 1).astype(np.float64)
    large = max_exact + (np.log(d / max_exact) / math.log(MAX_DISTANCE / max_exact)
                         * (N_BUCKETS - max_exact)).astype(np.int32)
    large = np.minimum(large, N_BUCKETS - 1)
    return np.where(dist < max_exact, dist, large).astype(np.int32)


def _prompt_bias_rows(rel_bias):
    j = np.arange(2 * N_STEPS)
    valid = j <= N_STEPS
    rows = []
    for (_, r) in PATTERNS:
        bucket = _t5_bucket(np.clip(N_STEPS - j, 0, N_STEPS) * r)
        rows.append(jnp.where(valid[:, None], rel_bias[bucket].astype(F32), NEG_INF))
    return jnp.stack(rows, 0).transpose(2, 0, 1)[:, :, None, :]


def _attn_prompt_kernel(q_ref, k_ref, v_ref, bias_ref, y_ref, acc_scr, m_scr, l_scr, *, seq):
    scale = HEAD_DIM ** -0.5
    nw = N_STEPS
    n_blocks = seq // nw

    for p, (_, r) in enumerate(PATTERNS):
        nb = n_blocks // r

        def rows(g):
            c, n = divmod(g, nb)
            start = c + n * nw * r
            return pl.ds(start, nw, stride=r) if r > 1 else pl.ds(start, nw)

        tile = pltpu.roll(jnp.broadcast_to(bias_ref[0, p], (nw, 2 * nw)), 0, 1,
                          stride=1, stride_axis=0)
        tile_cur = tile[:, nw:]
        qs = [q_ref[rows(g), :].astype(BF16) for g in range(n_blocks)]
        ks = [k_ref[rows(g), :].astype(BF16) for g in range(n_blocks)]
        vs = [v_ref[rows(g), :].astype(BF16) for g in range(n_blocks)]
        for g in range(n_blocks):
            if g % nb:
                kk = jnp.concatenate([ks[g - 1], ks[g]], axis=0)
                vv = jnp.concatenate([vs[g - 1], vs[g]], axis=0)
                bias = tile
            else:
                kk, vv, bias = ks[g], vs[g], tile_cur
            s = lax.dot_general(qs[g], kk, (((1,), (1,)), ((), ())), preferred_element_type=F32)
            s = s * scale + bias
            m = jnp.max(s, axis=1, keepdims=True)
            e = jnp.exp(s - m)
            l = jnp.sum(e, axis=1, keepdims=True)
            acc = jnp.dot(e.astype(BF16), vv, preferred_element_type=F32)
            acc_scr[p, rows(g), :] = acc
            m_scr[p, rows(g), :] = jnp.broadcast_to(m, (nw, HEAD_DIM))
            l_scr[p, rows(g), :] = jnp.broadcast_to(l, (nw, HEAD_DIM))

    rows_per_step = 256

    def merge(t, carry):
        rws = pl.ds(pl.multiple_of(t * rows_per_step, rows_per_step), rows_per_step)
        m0, m1, m2 = m_scr[0, rws, :], m_scr[1, rws, :], m_scr[2, rws, :]
        mx = jnp.maximum(jnp.maximum(m0, m1), m2)
        w0, w1, w2 = jnp.exp(m0 - mx), jnp.exp(m1 - mx), jnp.exp(m2 - mx)
        num = w0 * acc_scr[0, rws, :] + w1 * acc_scr[1, rws, :] + w2 * acc_scr[2, rws, :]
        den = w0 * l_scr[0, rws, :] + w1 * l_scr[1, rws, :] + w2 * l_scr[2, rws, :]
        y_ref[rws, :] = (num / den).astype(y_ref.dtype)
        return carry
    lax.fori_loop(0, seq // rows_per_step, merge, 0)


def _attn_prompt(q, k_all, v_all, layer, bias_rows, batch, seq):
    n_pat = len(PATTERNS)
    kernel = functools.partial(_attn_prompt_kernel, seq=seq)
    return pl.pallas_call(
        kernel,
        grid=(batch, N_HEADS_B),
        in_specs=[
            pl.BlockSpec((seq, HEAD_DIM), lambda b, h: (b, h)),
            pl.BlockSpec((None, seq, HEAD_DIM), lambda b, h: (layer, b, h)),
            pl.BlockSpec((None, seq, HEAD_DIM), lambda b, h: (layer, b, h)),
            pl.BlockSpec((1, n_pat, 1, 2 * N_STEPS), lambda b, h: (h, 0, 0, 0)),
        ],
        out_specs=pl.BlockSpec((seq, HEAD_DIM), lambda b, h: (b, h)),
        out_shape=jax.ShapeDtypeStruct((batch * seq, MIX_B), BF16),
        scratch_shapes=[pltpu.VMEM((n_pat, seq, HEAD_DIM), F32),
                        pltpu.VMEM((n_pat, seq, HEAD_DIM), F32),
                        pltpu.VMEM((n_pat, seq, HEAD_DIM), F32)],
        compiler_params=_params("parallel", "parallel"),
        name="attn_prompt",
    )(q, k_all, v_all, bias_rows)


def _sample_bias(rel_bias, wbuf, t_new):
    n = wbuf + t_new
    delta = np.arange(n)
    rows = []
    for (w, r) in PATTERNS:
        valid = (delta % r == 0) & (delta // r <= w // r)
        b = rel_bias[_t5_bucket(np.minimum(delta, w))].astype(F32)
        rows.append(jnp.where(valid[:, None], b, NEG_INF))
    f = jnp.stack(rows, 0).transpose(2, 0, 1)
    f_rev = jnp.pad(f[:, :, ::-1], ((0, 0), (0, 0), (0, t_new)), constant_values=NEG_INF)
    bias_c = jnp.stack([f_rev[:, :, t_new - 1 - t: t_new - 1 - t + wbuf] for t in range(t_new)], axis=2)
    bias_n = jnp.stack([f_rev[:, :, n - 1 - t: n - 1 - t + t_new] for t in range(t_new)], axis=2)
    return bias_c, bias_n


def _attn_sample_kernel(q_ref, kn_ref, vn_ref, kc_ref, vc_ref, bc_ref, bn_ref, *refs):
    y_ref, ko_ref, vo_ref = refs[-3:]
    scale = HEAD_DIM ** -0.5
    t_new = q_ref.shape[0]
    wbuf = kc_ref.shape[0]
    kc32 = kc_ref[...]
    vc32 = vc_ref[...]
    kn32 = kn_ref[...]
    vn32 = vn_ref[...]
    ko_ref[0:wbuf - t_new, :] = kc32[t_new:, :]
    ko_ref[wbuf - t_new:wbuf, :] = kn32
    vo_ref[0:wbuf - t_new, :] = vc32[t_new:, :]
    vo_ref[wbuf - t_new:wbuf, :] = vn32

    q = q_ref[...].astype(BF16)
    kc, vc, kn, vn = kc32.astype(BF16), vc32.astype(BF16), kn32.astype(BF16), vn32.astype(BF16)
    s_c = lax.dot_general(q, kc, (((1,), (1,)), ((), ())), preferred_element_type=F32) * scale
    s_n = lax.dot_general(q, kn, (((1,), (1,)), ((), ())), preferred_element_type=F32) * scale
    n_pat = len(PATTERNS)
    zc = [s_c + bc_ref[0, p] for p in range(n_pat)]
    zn = [s_n + bn_ref[0, p] for p in range(n_pat)]
    mx = None
    for z in zc + zn:
        zm = jnp.max(z, axis=1, keepdims=True)
        mx = zm if mx is None else jnp.maximum(mx, zm)
    pc = sum(jnp.exp(z - mx) for z in zc)
    pn = sum(jnp.exp(z - mx) for z in zn)
    den = jnp.sum(pc, axis=1, keepdims=True) + jnp.sum(pn, axis=1, keepdims=True)
    num = (jnp.dot(pc.astype(BF16), vc, preferred_element_type=F32)
           + jnp.dot(pn.astype(BF16), vn, preferred_element_type=F32))
    y_ref[...] = (num / den).astype(y_ref.dtype)


def _attn_sample(q, kn, vn, cache_k, cache_v, layer, bias_c, bias_n, batch, t_new, prev_k, prev_v):
    depth, _, wbuf, _ = cache_k.shape
    n_pat = len(PATTERNS)
    cache_spec = pl.BlockSpec((None, None, wbuf, HEAD_DIM), lambda b, h: (layer, b, 0, h))
    row_spec = pl.BlockSpec((t_new, HEAD_DIM), lambda b, h: (b, h))
    in_specs = [row_spec, row_spec, row_spec, cache_spec, cache_spec,
                pl.BlockSpec((1, n_pat, t_new, wbuf), lambda b, h: (h, 0, 0, 0)),
                pl.BlockSpec((1, n_pat, t_new, t_new), lambda b, h: (h, 0, 0, 0))]
    args = [q, kn, vn, cache_k, cache_v, bias_c, bias_n]
    aliases = {}
    if prev_k is not None:
        in_specs += [pl.BlockSpec(memory_space=pl.ANY), pl.BlockSpec(memory_space=pl.ANY)]
        args += [prev_k, prev_v]
        aliases = {7: 1, 8: 2}
    cache_shape = jax.ShapeDtypeStruct(cache_k.shape, F32)
    return pl.pallas_call(
        _attn_sample_kernel,
        grid=(batch, N_HEADS_B),
        in_specs=in_specs,
        out_specs=[row_spec, cache_spec, cache_spec],
        out_shape=[jax.ShapeDtypeStruct((batch * t_new, MIX_B), F32), cache_shape, cache_shape],
        input_output_aliases=aliases,
        compiler_params=_params("parallel", "parallel"),
        name="attn_sample",
    )(*args)


def kernel(x_prompt, x_sample, state_mlstm_C, state_mlstm_n, state_mlstm_m, cache_win_k, cache_win_v,
           g_ffn1, w_ffn1_gate, w_ffn1_up, w_ffn1_down, g_mix, w_in, b_igate, b_fgate, g_head, w_out,
           rel_bias, g_ffn2, w_ffn2_gate, w_ffn2_up, w_ffn2_down, g_final):
    bp, sp, d = x_prompt.shape
    bs, ts, _ = x_sample.shape
    depth = g_mix.shape[0]
    wbuf = cache_win_k.shape[2]
    mp, ms = bp * sp, bs * ts
    m_all = mp + ms
    d_ff = w_ffn1_gate.shape[2]
    tm_big = _row_tile(m_all, (1376, 1024, 512, 64))
    tm_down = _row_tile(m_all, (688, 512, 64))

    x = jnp.concatenate([x_prompt.reshape(mp, d), x_sample.reshape(ms, d)], axis=0)
    zc = jnp.zeros((bp, N_HEADS_A, DK_A, DV_A), F32)
    zn = jnp.zeros((bp, N_HEADS_A, DK_A), F32)
    zm = jnp.zeros((bp, N_HEADS_A), F32)
    bias_rows = _prompt_bias_rows(rel_bias)
    bias_c, bias_n = _sample_bias(rel_bias, wbuf, ts)
    cache_k = cache_win_k.reshape(depth, bs, wbuf, MIX_B)
    cache_v = cache_win_v.reshape(depth, bs, wbuf, MIX_B)
    b_off = A_WIDTH + GATE_COLS

    def ffn(x, g, wg, wu, wd, l):
        h = _rmsnorm(x, g[l], BF16)
        a = _ffn_up(h, wg, wu, l, tm_big)
        wd16 = _cast_layer(wd, l)
        w_spec = pl.BlockSpec((d_ff, 256), lambda i, j: (0, j))
        return _matmul_res(a, w_spec, wd16, x, 0.5, tm_down, 256)

    p_c, p_n, p_m, s_c, s_n, s_m = [], [], [], [], [], []
    pk = pv = sk = sv = None
    for l in range(depth):
        x = ffn(x, g_ffn1, w_ffn1_gate, w_ffn1_up, w_ffn1_down, l)

        h = _rmsnorm(x, g_mix[l], BF16)
        pa = _matmul_w32(h, w_in, l, 0, A_WIDTH, 256, tm_big)
        gates = _matmul_w32(h, w_in, l, A_WIDTH // LANES, LANES, LANES, tm_big)
        gate_bias = jnp.pad(jnp.concatenate([b_igate[l], b_fgate[l]]),
                            (0, LANES - GATE_COLS)).reshape(1, LANES)
        wb = w_in[l][:, b_off:].astype(BF16)

        ya_p, c_, n_, m_ = _mlstm(pa, gates, gate_bias, g_head, l, zc, zn, zm,
                                  0, bp, sp, PROMPT_CHUNK, BF16)
        p_c.append(c_); p_n.append(n_); p_m.append(m_)
        ya_s, c_, n_, m_ = _mlstm(pa, gates, gate_bias, g_head, l, state_mlstm_C[l], state_mlstm_n[l],
                                  state_mlstm_m[l], mp, bs, ts, ts, F32)
        s_c.append(c_); s_n.append(n_); s_m.append(m_)

        q_p, pk, pv = _qkv_prompt(h, wb, l, depth, mp, pk, pv)
        yb_p = _attn_prompt(q_p, pk, pv, l, bias_rows, bp, sp)
        q_s, k_s, v_s = _qkv_sample(h, wb, mp, ms)
        yb_s, sk, sv = _attn_sample(q_s, k_s, v_s, cache_k, cache_v, l, bias_c, bias_n, bs, ts, sk, sv)

        y = jnp.concatenate([jnp.concatenate([ya_p, yb_p], axis=1),
                             jnp.concatenate([ya_s, yb_s], axis=1).astype(BF16)], axis=0)
        wo_spec = pl.BlockSpec((None, d, 256), lambda i, j, l=l: (l, 0, j))
        x = _matmul_res(y, wo_spec, w_out, x, 1.0, tm_big, 256)

        x = ffn(x, g_ffn2, w_ffn2_gate, w_ffn2_up, w_ffn2_down, l)

    yp = _rmsnorm(x, g_final, F32, 0, mp).reshape(bp, sp, d)
    ys = _rmsnorm(x, g_final, F32, mp, ms).reshape(bs, ts, d)
    return (yp, ys, jnp.stack(p_c), jnp.stack(p_n), jnp.stack(p_m),
            pk.reshape(depth, bp, sp, N_HEADS_B, HEAD_DIM), pv.reshape(depth, bp, sp, N_HEADS_B, HEAD_DIM),
            jnp.stack(s_c), jnp.stack(s_n), jnp.stack(s_m),
            sk.reshape(depth, bs, wbuf, N_HEADS_B, HEAD_DIM), sv.reshape(depth, bs, wbuf, N_HEADS_B, HEAD_DIM))
```

```python
import functools
import math

import numpy as np
import jax
import jax.numpy as jnp
from jax import lax
from jax.experimental import pallas as pl
from jax.experimental.pallas import tpu as pltpu

F32 = jnp.float32
BF16 = jnp.bfloat16

D_MODEL = 4096
N_HEADS_A = 8
DK_A = 128
DV_A = 256
MIX_A = N_HEADS_A * DV_A
N_HEADS_B = 16
HEAD_DIM = 128
MIX_B = N_HEADS_B * HEAD_DIM
PATTERNS = ((128, 1), (512, 4), (2048, 16))
N_STEPS = 128
N_BUCKETS = 32
MAX_DISTANCE = 2048
EPS = 1e-6
NEG_INF = -1e30
A_WIDTH = 2 * N_HEADS_A * DK_A + 2 * MIX_A
GATE_COLS = 2 * N_HEADS_A
LANES = 128
HEAD_GROUP = 8
PROMPT_CHUNK = 256

VMEM_LIMIT = 56 * 1024 * 1024


def _params(*sem):
    return pltpu.CompilerParams(dimension_semantics=sem, vmem_limit_bytes=VMEM_LIMIT)


def _row_tile(m, candidates):
    for t in candidates:
        if m % t == 0:
            return t
    raise ValueError(f"no row tile for {m} rows among {candidates}")


def _rmsnorm_kernel(x_ref, g_ref, o_ref):
    x = x_ref[...]
    ms = jnp.mean(x * x, axis=-1, keepdims=True)
    o_ref[...] = (x * lax.rsqrt(ms + EPS) * g_ref[...]).astype(o_ref.dtype)


def _rmsnorm(x, g, out_dtype, row_start=0, n_rows=None):
    m, d = x.shape
    n_rows = m if n_rows is None else n_rows
    tm = _row_tile(math.gcd(n_rows, row_start) if row_start else n_rows, (256, 192, 128, 64))
    off = row_start // tm
    return pl.pallas_call(
        _rmsnorm_kernel,
        grid=(n_rows // tm,),
        in_specs=[pl.BlockSpec((tm, d), lambda i: (off + i, 0)),
                  pl.BlockSpec((1, d), lambda i: (0, 0))],
        out_specs=pl.BlockSpec((tm, d), lambda i: (i, 0)),
        out_shape=jax.ShapeDtypeStruct((n_rows, d), out_dtype),
        compiler_params=_params("parallel"),
        name="rmsnorm",
    )(x, g.reshape(1, d))


def _attn_cols_kernel(w_ref, tail_ref, o_ref, *, shift):
    w = jnp.concatenate([w_ref[...], tail_ref[...]], axis=1)
    o_ref[...] = w[:, shift:shift + o_ref.shape[1]].astype(o_ref.dtype)


def _attn_cols_bf16(w_in, layer, rows_per_step=256):
    _, k, n = w_in.shape
    width = n - A_WIDTH - GATE_COLS
    return pl.pallas_call(
        functools.partial(_attn_cols_kernel, shift=GATE_COLS),
        grid=(k // rows_per_step,),
        in_specs=[pl.BlockSpec((None, rows_per_step, width), lambda i: (layer, i, A_WIDTH // width)),
                  pl.BlockSpec((None, rows_per_step, LANES), lambda i: (layer, i, (A_WIDTH + width) // LANES))],
        out_specs=pl.BlockSpec((rows_per_step, width), lambda i: (i, 0)),
        out_shape=jax.ShapeDtypeStruct((k, width), BF16),
        compiler_params=_params("parallel"),
        name="attn_cols_bf16",
    )(w_in, w_in)


def _matmul_w32_kernel(x_ref, w_ref, o_ref):
    w = w_ref[...].astype(BF16)
    o_ref[...] = jnp.dot(x_ref[...], w, preferred_element_type=F32).astype(o_ref.dtype)


def _matmul_w32(x, w, layer, col_block0, n_cols, tn, tm):
    m, k = x.shape
    return pl.pallas_call(
        _matmul_w32_kernel,
        grid=(m // tm, n_cols // tn),
        in_specs=[pl.BlockSpec((tm, k), lambda i, j: (i, 0)),
                  pl.BlockSpec((None, k, tn), lambda i, j: (layer, 0, col_block0 + j))],
        out_specs=pl.BlockSpec((tm, tn), lambda i, j: (i, j)),
        out_shape=jax.ShapeDtypeStruct((m, n_cols), F32),
        compiler_params=_params("parallel", "arbitrary"),
        name="matmul_w32",
    )(x, w)


def _ffn_up_kernel(x_ref, wg_ref, wu_ref, wd_ref, o_ref, wd16_ref):
    x = x_ref[...]
    g = jnp.dot(x, wg_ref[...].astype(BF16), preferred_element_type=F32)
    u = jnp.dot(x, wu_ref[...].astype(BF16), preferred_element_type=F32)
    o_ref[...] = (g * jax.nn.sigmoid(g) * u).astype(o_ref.dtype)
    wd16_ref[...] = wd_ref[...].astype(wd16_ref.dtype)


def _ffn_up(x, wg, wu, wd, layer, tm, tn=256, wd_rows=64):
    m, k = x.shape
    n = wg.shape[2]
    n_j = n // tn
    n_slabs = wd.shape[1] // wd_rows
    assert wd.shape[1] % wd_rows == 0 and n_slabs <= (m // tm) * n_j

    def slab(i, j):
        return jnp.minimum(i * n_j + j, n_slabs - 1)

    return pl.pallas_call(
        _ffn_up_kernel,
        grid=(m // tm, n_j),
        in_specs=[pl.BlockSpec((tm, k), lambda i, j: (i, 0)),
                  pl.BlockSpec((None, k, tn), lambda i, j: (layer, 0, j)),
                  pl.BlockSpec((None, k, tn), lambda i, j: (layer, 0, j)),
                  pl.BlockSpec((None, wd_rows, wd.shape[2]), lambda i, j: (layer, slab(i, j), 0))],
        out_specs=[pl.BlockSpec((tm, tn), lambda i, j: (i, j)),
                   pl.BlockSpec((wd_rows, wd.shape[2]), lambda i, j: (slab(i, j), 0))],
        out_shape=[jax.ShapeDtypeStruct((m, n), BF16),
                   jax.ShapeDtypeStruct(wd.shape[1:], BF16)],
        compiler_params=_params("arbitrary", "arbitrary"),
        name="ffn_up",
    )(x, wg, wu, wd)


def _matmul_res_kernel(a_ref, w_ref, r_ref, o_ref, *, scale):
    acc = jnp.dot(a_ref[...], w_ref[...].astype(BF16), preferred_element_type=F32)
    o_ref[...] = r_ref[...] + scale * acc


def _matmul_res(a, w_spec, w, res, scale, tm, tn):
    m, _ = a.shape
    k = a.shape[1]
    n = res.shape[1]
    return pl.pallas_call(
        functools.partial(_matmul_res_kernel, scale=scale),
        grid=(m // tm, n // tn),
        in_specs=[pl.BlockSpec((tm, k), lambda i, j: (i, 0)),
                  w_spec,
                  pl.BlockSpec((tm, tn), lambda i, j: (i, j))],
        out_specs=pl.BlockSpec((tm, tn), lambda i, j: (i, j)),
        out_shape=jax.ShapeDtypeStruct((m, n), F32),
        compiler_params=_params("parallel", "arbitrary"),
        name="matmul_res",
    )(a, w, res)


def _qkv_kernel(x_ref, wq_ref, wk_ref, wv_ref, *refs):
    q_ref, k_ref, v_ref = refs[-3:]
    x = x_ref[...]
    q_ref[...] = jnp.dot(x, wq_ref[...], preferred_element_type=F32)
    k_ref[...] = jnp.dot(x, wk_ref[...], preferred_element_type=F32)
    v_ref[...] = jnp.dot(x, wv_ref[...], preferred_element_type=F32)


def _qkv_prompt(h, wb, layer, depth, n_rows, prev_k, prev_v, tm=1024, tn=256):
    k_dim = h.shape[1]
    nb = MIX_B // tn
    kv_shape = jax.ShapeDtypeStruct((depth, n_rows, MIX_B), F32)
    in_specs = [pl.BlockSpec((tm, k_dim), lambda i, j: (i, 0)),
                pl.BlockSpec((k_dim, tn), lambda i, j: (0, j)),
                pl.BlockSpec((k_dim, tn), lambda i, j: (0, nb + j)),
                pl.BlockSpec((k_dim, tn), lambda i, j: (0, 2 * nb + j))]
    args = [h, wb, wb, wb]
    aliases = {}
    if prev_k is not None:
        in_specs += [pl.BlockSpec(memory_space=pl.ANY), pl.BlockSpec(memory_space=pl.ANY)]
        args += [prev_k, prev_v]
        aliases = {4: 1, 5: 2}
    return pl.pallas_call(
        _qkv_kernel,
        grid=(n_rows // tm, nb),
        in_specs=in_specs,
        out_specs=[pl.BlockSpec((tm, tn), lambda i, j: (i, j)),
                   pl.BlockSpec((None, tm, tn), lambda i, j: (layer, i, j)),
                   pl.BlockSpec((None, tm, tn), lambda i, j: (layer, i, j))],
        out_shape=[jax.ShapeDtypeStruct((n_rows, MIX_B), F32), kv_shape, kv_shape],
        input_output_aliases=aliases,
        compiler_params=_params("parallel", "arbitrary"),
        name="qkv_prompt",
    )(*args)


def _qkv_sample(h, wb, row_start, n_rows, tn=512):
    k_dim = h.shape[1]
    nb = MIX_B // tn
    off = row_start // n_rows
    out = jax.ShapeDtypeStruct((n_rows, MIX_B), F32)
    return pl.pallas_call(
        _qkv_kernel,
        grid=(nb,),
        in_specs=[pl.BlockSpec((n_rows, k_dim), lambda j: (off, 0)),
                  pl.BlockSpec((k_dim, tn), lambda j: (0, j)),
                  pl.BlockSpec((k_dim, tn), lambda j: (0, nb + j)),
                  pl.BlockSpec((k_dim, tn), lambda j: (0, 2 * nb + j))],
        out_specs=[pl.BlockSpec((n_rows, tn), lambda j: (0, j))] * 3,
        out_shape=[out, out, out],
        compiler_params=_params("arbitrary"),
        name="qkv_sample",
    )(h, wb, wb, wb)


def _mlstm_kernel(q_ref, k_ref, v_ref, o_ref, g_ref, gb_ref, gh_ref, c0_ref, n0_ref, m0_ref,
                  y_ref, c_out_ref, n_out_ref, m_out_ref, c_scr, n_scr, m_scr, *, chunk, n_chunks):
    head = pl.program_id(1)
    c_idx = pl.program_id(2)
    L = chunk

    @pl.when(c_idx == 0)
    def _():
        c_scr[...] = c0_ref[0, 0]
        n_scr[...] = n0_ref[0]
        m_scr[...] = m0_ref[0]

    gates = g_ref[...] + gb_ref[...]
    lane = lax.broadcasted_iota(jnp.int32, gates.shape, 1)
    i_col = jnp.sum(jnp.where(lane == head, gates, 0.0), axis=1, keepdims=True)
    f_col = jnp.sum(jnp.where(lane == head + N_HEADS_A, gates, 0.0), axis=1, keepdims=True)
    lf_col = jax.nn.log_sigmoid(f_col)

    row = lax.broadcasted_iota(jnp.int32, (L, L), 0)
    col = lax.broadcasted_iota(jnp.int32, (L, L), 1)
    eye = row == col
    causal = col <= row
    lf_row = jnp.sum(jnp.where(eye, lf_col, 0.0), axis=0, keepdims=True)
    i_row = jnp.sum(jnp.where(eye, i_col, 0.0), axis=0, keepdims=True)
    b_col = jnp.sum(jnp.where(causal, lf_row, 0.0), axis=1, keepdims=True)
    b_row = jnp.sum(jnp.where(row <= col, lf_col, 0.0), axis=0, keepdims=True)

    m_prev = m_scr[...]
    dmat = jnp.where(causal, b_col - b_row + i_row, -jnp.inf)
    inter = b_col + m_prev
    mt = jnp.maximum(inter, jnp.max(dmat, axis=1, keepdims=True))

    q32 = q_ref[...]
    k32 = k_ref[...] * (DK_A ** -0.5)
    v32 = v_ref[...]
    q = q32.astype(BF16)
    k = k32.astype(BF16)
    v = v32.astype(BF16)
    c_prev = c_scr[...]
    n_prev = n_scr[...]

    s = lax.dot_general(q, k, (((1,), (1,)), ((), ())), preferred_element_type=F32)
    w = s * jnp.exp(dmat - mt)
    a = jnp.exp(inter - mt)
    num = (jnp.dot(w.astype(BF16), v, preferred_element_type=F32)
           + a * jnp.dot(q, c_prev.astype(BF16), preferred_element_type=F32))
    den = (jnp.sum(w, axis=1, keepdims=True)
           + a * jnp.sum(q32 * n_prev, axis=1, keepdims=True))
    h = num / jnp.maximum(jnp.abs(den), jnp.exp(-mt))

    m_new = mt[L - 1:L, :]
    b_last = b_col[L - 1:L, :]
    g = jnp.exp(b_last - b_col + i_col - m_new)
    dec = jnp.exp(b_last + m_prev - m_new)
    gk = g * k32
    c_new = dec * c_prev + lax.dot_general(gk.astype(BF16), v, (((0,), (0,)), ((), ())),
                                           preferred_element_type=F32)
    n_new = dec * n_prev + jnp.sum(gk, axis=0, keepdims=True)
    c_scr[...] = c_new
    n_scr[...] = n_new
    m_scr[...] = m_new

    hn = h * lax.rsqrt(jnp.mean(h * h, axis=-1, keepdims=True) + EPS)
    y_ref[...] = (hn * gh_ref[...] * jax.nn.sigmoid(o_ref[...])).astype(y_ref.dtype)

    @pl.when(c_idx == n_chunks - 1)
    def _():
        c_out_ref[0, 0] = c_new
        n_out_ref[0] = n_new
        m_out_ref[0] = m_new


def _mlstm(pa, gates, gate_bias, g_head, layer, c0, n0, m0, row_start, batch, seq, chunk, out_dtype):
    n_chunks = seq // chunk
    m_rows = batch * seq
    k_off = (N_HEADS_A * DK_A) // DK_A
    v_off = (2 * N_HEADS_A * DK_A) // DV_A
    o_off = (2 * N_HEADS_A * DK_A + MIX_A) // DV_A
    bh = batch * N_HEADS_A
    r0 = row_start // chunk

    def rows(b, h, c):
        return r0 + b * n_chunks + c

    kernel = functools.partial(_mlstm_kernel, chunk=chunk, n_chunks=n_chunks)
    y, c_fin, n_fin, m_fin = pl.pallas_call(
        kernel,
        grid=(batch, N_HEADS_A, n_chunks),
        in_specs=[
            pl.BlockSpec((chunk, DK_A), lambda b, h, c: (rows(b, h, c), h)),
            pl.BlockSpec((chunk, DK_A), lambda b, h, c: (rows(b, h, c), k_off + h)),
            pl.BlockSpec((chunk, DV_A), lambda b, h, c: (rows(b, h, c), v_off + h)),
            pl.BlockSpec((chunk, DV_A), lambda b, h, c: (rows(b, h, c), o_off + h)),
            pl.BlockSpec((chunk, LANES), lambda b, h, c: (rows(b, h, c), 0)),
            pl.BlockSpec((1, LANES), lambda b, h, c: (0, 0)),
            pl.BlockSpec((None, 1, DV_A), lambda b, h, c: (layer, 0, h)),
            pl.BlockSpec((1, 1, DK_A, DV_A), lambda b, h, c: (b, h, 0, 0)),
            pl.BlockSpec((1, 1, DK_A), lambda b, h, c: (b * N_HEADS_A + h, 0, 0)),
            pl.BlockSpec((1, 1, 1), lambda b, h, c: (b * N_HEADS_A + h, 0, 0)),
        ],
        out_specs=[
            pl.BlockSpec((chunk, DV_A), lambda b, h, c: (b * n_chunks + c, h)),
            pl.BlockSpec((1, 1, DK_A, DV_A), lambda b, h, c: (b, h, 0, 0)),
            pl.BlockSpec((1, 1, DK_A), lambda b, h, c: (b * N_HEADS_A + h, 0, 0)),
            pl.BlockSpec((1, 1, 1), lambda b, h, c: (b * N_HEADS_A + h, 0, 0)),
        ],
        out_shape=[
            jax.ShapeDtypeStruct((m_rows, MIX_A), out_dtype),
            jax.ShapeDtypeStruct((batch, N_HEADS_A, DK_A, DV_A), F32),
            jax.ShapeDtypeStruct((bh, 1, DK_A), F32),
            jax.ShapeDtypeStruct((bh, 1, 1), F32),
        ],
        scratch_shapes=[pltpu.VMEM((DK_A, DV_A), F32),
                        pltpu.VMEM((1, DK_A), F32),
                        pltpu.VMEM((1, 1), F32)],
        compiler_params=_params("parallel", "parallel", "arbitrary"),
        name="mlstm",
    )(pa, pa, pa, pa, gates, gate_bias, g_head.reshape(g_head.shape[0], 1, MIX_A),
      c0, n0.reshape(bh, 1, DK_A), m0.reshape(bh, 1, 1))
    return (y, c_fin, n_fin.reshape(batch, N_HEADS_A, DK_A), m_fin.reshape(batch, N_HEADS_A))


def _t5_bucket(dist):
    max_exact = N_BUCKETS // 2
    d = np.maximum(dist, 1).astype(np.float64)
    large = max_exact + (np.log(d / max_exact) / math.log(MAX_DISTANCE / max_exact)
                         * (N_BUCKETS - max_exact)).astype(np.int32)
    large = np.minimum(large, N_BUCKETS - 1)
    return np.where(dist < max_exact, dist, large).astype(np.int32)


def _prompt_bias_rows(rel_bias):
    j = np.arange(2 * N_STEPS)
    valid = j <= N_STEPS
    rows = []
    for (_, r) in PATTERNS:
        bucket = _t5_bucket(np.clip(N_STEPS - j, 0, N_STEPS) * r)
        rows.append(jnp.where(valid[:, None], rel_bias[bucket].astype(F32), NEG_INF))
    return jnp.stack(rows, 0).transpose(2, 0, 1)[:, :, None, :]


def _attn_prompt_kernel(q_ref, k_ref, v_ref, bias_ref, y_ref, acc_scr, m_scr, l_scr, *, seq):
    scale = HEAD_DIM ** -0.5
    nw = N_STEPS
    n_blocks = seq // nw

    for p, (_, r) in enumerate(PATTERNS):
        nb = n_blocks // r

        def rows(g):
            c, n = divmod(g, nb)
            start = c + n * nw * r
            return pl.ds(start, nw, stride=r) if r > 1 else pl.ds(start, nw)

        tile = pltpu.roll(jnp.broadcast_to(bias_ref[0, p], (nw, 2 * nw)), 0, 1,
                          stride=1, stride_axis=0)
        tile_cur = tile[:, nw:]
        qs = [q_ref[rows(g), :].astype(BF16) for g in range(n_blocks)]
        ks = [k_ref[rows(g), :].astype(BF16) for g in range(n_blocks)]
        vs = [v_ref[rows(g), :].astype(BF16) for g in range(n_blocks)]
        for g in range(n_blocks):
            if g % nb:
                kk = jnp.concatenate([ks[g - 1], ks[g]], axis=0)
                vv = jnp.concatenate([vs[g - 1], vs[g]], axis=0)
                bias = tile
            else:
                kk, vv, bias = ks[g], vs[g], tile_cur
            s = lax.dot_general(qs[g], kk, (((1,), (1,)), ((), ())), preferred_element_type=F32)
            s = s * scale + bias
            m = jnp.max(s, axis=1, keepdims=True)
            e = jnp.exp(s - m)
            l = jnp.sum(e, axis=1, keepdims=True)
            acc = jnp.dot(e.astype(BF16), vv, preferred_element_type=F32)
            acc_scr[p, rows(g), :] = acc
            m_scr[p, rows(g), :] = jnp.broadcast_to(m, (nw, HEAD_DIM))
            l_scr[p, rows(g), :] = jnp.broadcast_to(l, (nw, HEAD_DIM))

    rows_per_step = 256

    def merge(t, carry):
        rws = pl.ds(pl.multiple_of(t * rows_per_step, rows_per_step), rows_per_step)
        m0, m1, m2 = m_scr[0, rws, :], m_scr[1, rws, :], m_scr[2, rws, :]
        mx = jnp.maximum(jnp.maximum(m0, m1), m2)
        w0, w1, w2 = jnp.exp(m0 - mx), jnp.exp(m1 - mx), jnp.exp(m2 - mx)
        num = w0 * acc_scr[0, rws, :] + w1 * acc_scr[1, rws, :] + w2 * acc_scr[2, rws, :]
        den = w0 * l_scr[0, rws, :] + w1 * l_scr[1, rws, :] + w2 * l_scr[2, rws, :]
        y_ref[rws, :] = (num / den).astype(y_ref.dtype)
        return carry
    lax.fori_loop(0, seq // rows_per_step, merge, 0)


def _attn_prompt(q, k_all, v_all, layer, bias_rows, batch, seq):
    n_pat = len(PATTERNS)
    kernel = functools.partial(_attn_prompt_kernel, seq=seq)
    return pl.pallas_call(
        kernel,
        grid=(batch, N_HEADS_B),
        in_specs=[
            pl.BlockSpec((seq, HEAD_DIM), lambda b, h: (b, h)),
            pl.BlockSpec((None, seq, HEAD_DIM), lambda b, h: (layer, b, h)),
            pl.BlockSpec((None, seq, HEAD_DIM), lambda b, h: (layer, b, h)),
            pl.BlockSpec((1, n_pat, 1, 2 * N_STEPS), lambda b, h: (h, 0, 0, 0)),
        ],
        out_specs=pl.BlockSpec((seq, HEAD_DIM), lambda b, h: (b, h)),
        out_shape=jax.ShapeDtypeStruct((batch * seq, MIX_B), BF16),
        scratch_shapes=[pltpu.VMEM((n_pat, seq, HEAD_DIM), F32),
                        pltpu.VMEM((n_pat, seq, HEAD_DIM), F32),
                        pltpu.VMEM((n_pat, seq, HEAD_DIM), F32)],
        compiler_params=_params("parallel", "parallel"),
        name="attn_prompt",
    )(q, k_all, v_all, bias_rows)


def _sample_bias(rel_bias, wbuf, t_new):
    n = wbuf + t_new
    delta = np.arange(n)
    rows = []
    for (w, r) in PATTERNS:
        valid = (delta % r == 0) & (delta // r <= w // r)
        b = rel_bias[_t5_bucket(np.minimum(delta, w))].astype(F32)
        rows.append(jnp.where(valid[:, None], b, NEG_INF))
    f = jnp.stack(rows, 0).transpose(2, 0, 1)
    f_rev = jnp.pad(f[:, :, ::-1], ((0, 0), (0, 0), (0, t_new)), constant_values=NEG_INF)
    bias_c = jnp.stack([f_rev[:, :, t_new - 1 - t: t_new - 1 - t + wbuf] for t in range(t_new)], axis=2)
    bias_n = jnp.stack([f_rev[:, :, n - 1 - t: n - 1 - t + t_new] for t in range(t_new)], axis=2)
    return bias_c, bias_n


def _attn_sample_kernel(q_ref, kn_ref, vn_ref, kc_ref, vc_ref, kx_ref, vx_ref, bc_ref, bn_ref, *refs,
                        n_chunks):
    y_ref, ko_ref, vo_ref, m_scr, l_scr, acc_scr = refs[-6:]
    scale = HEAD_DIM ** -0.5
    n_pat = len(PATTERNS)
    chunk = pl.program_id(1)
    t_new = q_ref.shape[0]
    rows = kc_ref.shape[0]
    shift = t_new * N_HEADS_B
    tokens = rows // N_HEADS_B

    ko_ref[0:rows - shift, :] = kc_ref[shift:rows, :]
    vo_ref[0:rows - shift, :] = vc_ref[shift:rows, :]

    @pl.when(chunk < n_chunks - 1)
    def _():
        ko_ref[rows - shift:rows, :] = kx_ref[...]
        vo_ref[rows - shift:rows, :] = vx_ref[...]

    @pl.when(chunk == n_chunks - 1)
    def _():
        for h in range(N_HEADS_B):
            dst = pl.ds(rows - shift + h, t_new, stride=N_HEADS_B)
            ko_ref[dst, :] = kn_ref[:, h * HEAD_DIM:(h + 1) * HEAD_DIM]
            vo_ref[dst, :] = vn_ref[:, h * HEAD_DIM:(h + 1) * HEAD_DIM]

    @pl.when(chunk == 0)
    def _():
        m_scr[...] = jnp.full(m_scr.shape, NEG_INF, F32)
        l_scr[...] = jnp.zeros(l_scr.shape, F32)
        acc_scr[...] = jnp.zeros(acc_scr.shape, F32)

    def update(h, q, k, v, biases):
        s = lax.dot_general(q, k, (((1,), (1,)), ((), ())), preferred_element_type=F32) * scale
        zs = [s + b for b in biases]
        zmax = jnp.max(zs[0], axis=1, keepdims=True)
        for z in zs[1:]:
            zmax = jnp.maximum(zmax, jnp.max(z, axis=1, keepdims=True))
        m_old = m_scr[h]
        m_new = jnp.maximum(m_old, zmax)
        alpha = jnp.exp(m_old - m_new)
        pt = jnp.exp(zs[0] - m_new)
        for z in zs[1:]:
            pt = pt + jnp.exp(z - m_new)
        m_scr[h] = m_new
        l_scr[h] = alpha * l_scr[h] + jnp.sum(pt, axis=1, keepdims=True)
        acc_scr[h] = alpha * acc_scr[h] + jnp.dot(pt.astype(BF16), v, preferred_element_type=F32)

    for h in range(N_HEADS_B):
        head_rows = pl.ds(h, tokens, stride=N_HEADS_B)
        q = q_ref[:, h * HEAD_DIM:(h + 1) * HEAD_DIM].astype(BF16)
        update(h, q, kc_ref[head_rows, :].astype(BF16), vc_ref[head_rows, :].astype(BF16),
               [bc_ref[h, p] for p in range(n_pat)])

    @pl.when(chunk == n_chunks - 1)
    def _():
        for h in range(N_HEADS_B):
            cols = slice(h * HEAD_DIM, (h + 1) * HEAD_DIM)
            q = q_ref[:, cols].astype(BF16)
            update(h, q, kn_ref[:, cols].astype(BF16), vn_ref[:, cols].astype(BF16),
                   [bn_ref[h, p] for p in range(n_pat)])
            y_ref[:, cols] = (acc_scr[h] / l_scr[h]).astype(y_ref.dtype)


def _attn_sample(q, kn, vn, cache_k, cache_v, layer, bias_c, bias_n, batch, t_new, prev_k, prev_v,
                 n_chunks=4):
    depth, _, all_rows, _ = cache_k.shape
    rows = all_rows // n_chunks
    shift = t_new * N_HEADS_B
    n_pat = len(PATTERNS)
    chunk_spec = pl.BlockSpec((None, None, rows, HEAD_DIM), lambda b, c: (layer, b, c, 0))
    next_spec = pl.BlockSpec((None, None, shift, HEAD_DIM),
                             lambda b, c: (layer, b, jnp.minimum(c + 1, n_chunks - 1) * (rows // shift), 0))
    row_spec = pl.BlockSpec((t_new, MIX_B), lambda b, c: (b, 0))
    in_specs = [row_spec, row_spec, row_spec, chunk_spec, chunk_spec, next_spec, next_spec,
                pl.BlockSpec((N_HEADS_B, n_pat, t_new, rows // N_HEADS_B), lambda b, c: (0, 0, 0, c)),
                pl.BlockSpec((N_HEADS_B, n_pat, t_new, t_new), lambda b, c: (0, 0, 0, 0))]
    args = [q, kn, vn, cache_k, cache_v, cache_k, cache_v, bias_c, bias_n]
    aliases = {}
    if prev_k is not None:
        in_specs += [pl.BlockSpec(memory_space=pl.ANY), pl.BlockSpec(memory_space=pl.ANY)]
        args += [prev_k, prev_v]
        aliases = {9: 1, 10: 2}
    cache_shape = jax.ShapeDtypeStruct(cache_k.shape, F32)
    return pl.pallas_call(
        functools.partial(_attn_sample_kernel, n_chunks=n_chunks),
        grid=(batch, n_chunks),
        in_specs=in_specs,
        out_specs=[row_spec, chunk_spec, chunk_spec],
        out_shape=[jax.ShapeDtypeStruct((batch * t_new, MIX_B), F32), cache_shape, cache_shape],
        scratch_shapes=[pltpu.VMEM((N_HEADS_B, t_new, 1), F32),
                        pltpu.VMEM((N_HEADS_B, t_new, 1), F32),
                        pltpu.VMEM((N_HEADS_B, t_new, HEAD_DIM), F32)],
        input_output_aliases=aliases,
        compiler_params=_params("parallel", "arbitrary"),
        name="attn_sample",
    )(*args)


def kernel(x_prompt, x_sample, state_mlstm_C, state_mlstm_n, state_mlstm_m, cache_win_k, cache_win_v,
           g_ffn1, w_ffn1_gate, w_ffn1_up, w_ffn1_down, g_mix, w_in, b_igate, b_fgate, g_head, w_out,
           rel_bias, g_ffn2, w_ffn2_gate, w_ffn2_up, w_ffn2_down, g_final):
    bp, sp, d = x_prompt.shape
    bs, ts, _ = x_sample.shape
    depth = g_mix.shape[0]
    wbuf = cache_win_k.shape[2]
    mp, ms = bp * sp, bs * ts
    m_all = mp + ms
    d_ff = w_ffn1_gate.shape[2]
    tm_big = _row_tile(m_all, (1376, 1024, 512, 64))
    tm_down = _row_tile(m_all, (688, 512, 64))

    x = jnp.concatenate([x_prompt.reshape(mp, d), x_sample.reshape(ms, d)], axis=0)
    zc = jnp.zeros((bp, N_HEADS_A, DK_A, DV_A), F32)
    zn = jnp.zeros((bp, N_HEADS_A, DK_A), F32)
    zm = jnp.zeros((bp, N_HEADS_A), F32)
    bias_rows = _prompt_bias_rows(rel_bias)
    bias_c, bias_n = _sample_bias(rel_bias, wbuf, ts)
    cache_k = cache_win_k.reshape(depth, bs, wbuf * N_HEADS_B, HEAD_DIM)
    cache_v = cache_win_v.reshape(depth, bs, wbuf * N_HEADS_B, HEAD_DIM)

    def ffn(x, g, wg, wu, wd, l):
        h = _rmsnorm(x, g[l], BF16)
        a, wd16 = _ffn_up(h, wg, wu, wd, l, tm_big)
        w_spec = pl.BlockSpec((d_ff, 256), lambda i, j: (0, j))
        return _matmul_res(a, w_spec, wd16, x, 0.5, tm_down, 256)

    p_c, p_n, p_m, s_c, s_n, s_m = [], [], [], [], [], []
    pk = pv = sk = sv = None
    for l in range(depth):
        x = ffn(x, g_ffn1, w_ffn1_gate, w_ffn1_up, w_ffn1_down, l)

        h = _rmsnorm(x, g_mix[l], BF16)
        pa = _matmul_w32(h, w_in, l, 0, A_WIDTH, 256, tm_big)
        gates = _matmul_w32(h, w_in, l, A_WIDTH // LANES, LANES, LANES, tm_big)
        gate_bias = jnp.pad(jnp.concatenate([b_igate[l], b_fgate[l]]),
                            (0, LANES - GATE_COLS)).reshape(1, LANES)
        wb = _attn_cols_bf16(w_in, l)

        ya_p, c_, n_, m_ = _mlstm(pa, gates, gate_bias, g_head, l, zc, zn, zm,
                                  0, bp, sp, PROMPT_CHUNK, BF16)
        p_c.append(c_); p_n.append(n_); p_m.append(m_)
        ya_s, c_, n_, m_ = _mlstm(pa, gates, gate_bias, g_head, l, state_mlstm_C[l], state_mlstm_n[l],
                                  state_mlstm_m[l], mp, bs, ts, ts, F32)
        s_c.append(c_); s_n.append(n_); s_m.append(m_)

        q_p, pk, pv = _qkv_prompt(h, wb, l, depth, mp, pk, pv)
        yb_p = _attn_prompt(q_p, pk, pv, l, bias_rows, bp, sp)
        q_s, k_s, v_s = _qkv_sample(h, wb, mp, ms)
        yb_s, sk, sv = _attn_sample(q_s, k_s, v_s, cache_k, cache_v, l, bias_c, bias_n, bs, ts, sk, sv)

        y = jnp.concatenate([jnp.concatenate([ya_p, yb_p], axis=1),
                             jnp.concatenate([ya_s, yb_s], axis=1).astype(BF16)], axis=0)
        wo_spec = pl.BlockSpec((None, d, 256), lambda i, j, l=l: (l, 0, j))
        x = _matmul_res(y, wo_spec, w_out, x, 1.0, tm_big, 256)

        x = ffn(x, g_ffn2, w_ffn2_gate, w_ffn2_up, w_ffn2_down, l)

    yp = _rmsnorm(x, g_final, F32, 0, mp).reshape(bp, sp, d)
    ys = _rmsnorm(x, g_final, F32, mp, ms).reshape(bs, ts, d)
    return (yp, ys, jnp.stack(p_c), jnp.stack(p_n), jnp.stack(p_m),
            pk.reshape(depth, bp, sp, N_HEADS_B, HEAD_DIM), pv.reshape(depth, bp, sp, N_HEADS_B, HEAD_DIM),
            jnp.stack(s_c), jnp.stack(s_n), jnp.stack(s_m),
            sk.reshape(depth, bs, wbuf, N_HEADS_B, HEAD_DIM), sv.reshape(depth, bs, wbuf, N_HEADS_B, HEAD_DIM))
```

```python
import functools
import math

import numpy as np
import jax
import jax.numpy as jnp
from jax import lax
from jax.experimental import pallas as pl
from jax.experimental.pallas import tpu as pltpu

F32 = jnp.float32
BF16 = jnp.bfloat16

D_MODEL = 4096
N_HEADS_A = 8
DK_A = 128
DV_A = 256
MIX_A = N_HEADS_A * DV_A
N_HEADS_B = 16
HEAD_DIM = 128
MIX_B = N_HEADS_B * HEAD_DIM
PATTERNS = ((128, 1), (512, 4), (2048, 16))
N_STEPS = 128
N_BUCKETS = 32
MAX_DISTANCE = 2048
EPS = 1e-6
NEG_INF = -1e30
A_WIDTH = 2 * N_HEADS_A * DK_A + 2 * MIX_A
GATE_COLS = 2 * N_HEADS_A
LANES = 128
SUBLANES = 8
PROMPT_CHUNK = 256

VMEM_LIMIT = 56 * 1024 * 1024


def _params(*sem):
    return pltpu.CompilerParams(dimension_semantics=sem, vmem_limit_bytes=VMEM_LIMIT)


def _row_tile(m, candidates):
    for t in candidates:
        if m % t == 0:
            return t
    raise ValueError(f"no row tile for {m} rows among {candidates}")


def _inv_rms(ssq_ref, d):
    return lax.rsqrt(ssq_ref[:, 0:1] * (1.0 / d) + EPS)


def _rmsnorm_kernel(x_ref, g_ref, o_ref):
    x = x_ref[...]
    ms = jnp.mean(x * x, axis=-1, keepdims=True)
    o_ref[...] = (x * lax.rsqrt(ms + EPS) * g_ref[...]).astype(o_ref.dtype)


def _rmsnorm(x, g, out_dtype, row_start, n_rows):
    _, d = x.shape
    tm = _row_tile(math.gcd(n_rows, row_start) if row_start else n_rows, (256, 192, 128, 64))
    off = row_start // tm
    return pl.pallas_call(
        _rmsnorm_kernel,
        grid=(n_rows // tm,),
        in_specs=[pl.BlockSpec((tm, d), lambda i: (off + i, 0)),
                  pl.BlockSpec((1, d), lambda i: (0, 0))],
        out_specs=pl.BlockSpec((tm, d), lambda i: (i, 0)),
        out_shape=jax.ShapeDtypeStruct((n_rows, d), out_dtype),
        compiler_params=_params("parallel"),
        name="rmsnorm",
    )(x, g.reshape(1, d))


def _gain_ssq_kernel(x_ref, g_ref, xg_ref, ssq_ref):
    x = x_ref[...]
    xg_ref[...] = (x * g_ref[...]).astype(xg_ref.dtype)
    ssq_ref[...] = jnp.broadcast_to(jnp.sum(x * x, axis=-1, keepdims=True), ssq_ref.shape)


def _gain_ssq(x, g):
    m, d = x.shape
    tm = _row_tile(m, (256, 192, 128, 64))
    return pl.pallas_call(
        _gain_ssq_kernel,
        grid=(m // tm,),
        in_specs=[pl.BlockSpec((tm, d), lambda i: (i, 0)),
                  pl.BlockSpec((1, d), lambda i: (0, 0))],
        out_specs=[pl.BlockSpec((tm, d), lambda i: (i, 0)),
                   pl.BlockSpec((tm, LANES), lambda i: (i, 0))],
        out_shape=[jax.ShapeDtypeStruct((m, d), BF16), jax.ShapeDtypeStruct((m, LANES), F32)],
        compiler_params=_params("parallel"),
        name="gain_ssq",
    )(x, g.reshape(1, d))


_NT = (((1,), (1,)), ((), ()))


def _proj_nt_kernel(x_ref, ssq_ref, w_ref, o_ref):
    w = w_ref[...].astype(BF16)
    acc = lax.dot_general(x_ref[...], w, _NT, preferred_element_type=F32)
    o_ref[...] = acc * _inv_rms(ssq_ref, x_ref.shape[1])


def _proj_nt(xg, ssq, wt, layer, row_block0, n_cols, tn, tm):
    m, k = xg.shape
    return pl.pallas_call(
        _proj_nt_kernel,
        grid=(m // tm, n_cols // tn),
        in_specs=[pl.BlockSpec((tm, k), lambda i, j: (i, 0)),
                  pl.BlockSpec((tm, LANES), lambda i, j: (i, 0)),
                  pl.BlockSpec((None, tn, k), lambda i, j: (layer, row_block0 + j, 0))],
        out_specs=pl.BlockSpec((tm, tn), lambda i, j: (i, j)),
        out_shape=jax.ShapeDtypeStruct((m, n_cols), F32),
        compiler_params=_params("parallel", "arbitrary"),
        name="proj_nt",
    )(xg, ssq, wt)


def _qkv_kernel(x_ref, ssq_ref, wq_ref, wk_ref, wv_ref, *refs):
    q_ref, k_ref, v_ref = refs[-3:]
    x = x_ref[...]
    inv = _inv_rms(ssq_ref, x_ref.shape[1])
    for w_ref, o_ref in ((wq_ref, q_ref), (wk_ref, k_ref), (wv_ref, v_ref)):
        acc = lax.dot_general(x, w_ref[...].astype(BF16), _NT, preferred_element_type=F32)
        o_ref[...] = acc * inv


def _qkv_weight_specs(wt2d_rows, layer, tn, k_dim):
    base = layer * wt2d_rows + A_WIDTH + GATE_COLS
    specs = []
    for part in range(3):
        def index(*ids, part=part):
            j = ids[-1]
            return (pl.multiple_of(base + part * MIX_B + j * tn, SUBLANES), 0)
        specs.append(pl.BlockSpec((pl.Element(tn), pl.Element(k_dim)), index))
    return specs


def _qkv_prompt(xg, ssq, wt, layer, depth, n_rows, prev_k, prev_v, tm=1024, tn=256):
    k_dim = xg.shape[1]
    nb = MIX_B // tn
    n_all = wt.shape[1]
    kv_shape = jax.ShapeDtypeStruct((depth, n_rows, MIX_B), F32)
    in_specs = ([pl.BlockSpec((tm, k_dim), lambda i, j: (i, 0)),
                 pl.BlockSpec((tm, LANES), lambda i, j: (i, 0))]
                + _qkv_weight_specs(n_all, layer, tn, k_dim))
    wt2d = wt.reshape(depth * n_all, k_dim)
    args = [xg, ssq, wt2d, wt2d, wt2d]
    aliases = {}
    if prev_k is not None:
        in_specs += [pl.BlockSpec(memory_space=pl.ANY), pl.BlockSpec(memory_space=pl.ANY)]
        args += [prev_k, prev_v]
        aliases = {5: 1, 6: 2}
    return pl.pallas_call(
        _qkv_kernel,
        grid=(n_rows // tm, nb),
        in_specs=in_specs,
        out_specs=[pl.BlockSpec((tm, tn), lambda i, j: (i, j)),
                   pl.BlockSpec((None, tm, tn), lambda i, j: (layer, i, j)),
                   pl.BlockSpec((None, tm, tn), lambda i, j: (layer, i, j))],
        out_shape=[jax.ShapeDtypeStruct((n_rows, MIX_B), F32), kv_shape, kv_shape],
        input_output_aliases=aliases,
        compiler_params=_params("parallel", "arbitrary"),
        name="qkv_prompt",
    )(*args)


def _qkv_sample(xg, ssq, wt, layer, row_start, n_rows, tn=256):
    k_dim = xg.shape[1]
    nb = MIX_B // tn
    depth, n_all, _ = wt.shape
    off = row_start // n_rows
    out = jax.ShapeDtypeStruct((n_rows, MIX_B), F32)
    wt2d = wt.reshape(depth * n_all, k_dim)
    return pl.pallas_call(
        _qkv_kernel,
        grid=(nb,),
        in_specs=([pl.BlockSpec((n_rows, k_dim), lambda j: (off, 0)),
                   pl.BlockSpec((n_rows, LANES), lambda j: (off, 0))]
                  + _qkv_weight_specs(n_all, layer, tn, k_dim)),
        out_specs=[pl.BlockSpec((n_rows, tn), lambda j: (0, j))] * 3,
        out_shape=[out, out, out],
        compiler_params=_params("arbitrary"),
        name="qkv_sample",
    )(xg, ssq, wt2d, wt2d, wt2d)


def _ffn_up_kernel(x_ref, ssq_ref, wg_ref, wu_ref, wd_ref, o_ref, wd16_ref):
    x = x_ref[...]
    inv = _inv_rms(ssq_ref, x_ref.shape[1])
    g = jnp.dot(x, wg_ref[...].astype(BF16), preferred_element_type=F32) * inv
    u = jnp.dot(x, wu_ref[...].astype(BF16), preferred_element_type=F32) * inv
    o_ref[...] = (g * jax.nn.sigmoid(g) * u).astype(o_ref.dtype)
    wd16_ref[...] = wd_ref[...].astype(wd16_ref.dtype)


def _ffn_up(xg, ssq, wg, wu, wd, layer, tm, tn=256, wd_rows=64):
    m, k = xg.shape
    n = wg.shape[2]
    n_j = n // tn
    n_slabs = wd.shape[1] // wd_rows
    assert wd.shape[1] % wd_rows == 0 and n_slabs <= (m // tm) * n_j

    def slab(i, j):
        return jnp.minimum(i * n_j + j, n_slabs - 1)

    return pl.pallas_call(
        _ffn_up_kernel,
        grid=(m // tm, n_j),
        in_specs=[pl.BlockSpec((tm, k), lambda i, j: (i, 0)),
                  pl.BlockSpec((tm, LANES), lambda i, j: (i, 0)),
                  pl.BlockSpec((None, k, tn), lambda i, j: (layer, 0, j)),
                  pl.BlockSpec((None, k, tn), lambda i, j: (layer, 0, j)),
                  pl.BlockSpec((None, wd_rows, wd.shape[2]), lambda i, j: (layer, slab(i, j), 0))],
        out_specs=[pl.BlockSpec((tm, tn), lambda i, j: (i, j)),
                   pl.BlockSpec((wd_rows, wd.shape[2]), lambda i, j: (slab(i, j), 0))],
        out_shape=[jax.ShapeDtypeStruct((m, n), BF16),
                   jax.ShapeDtypeStruct(wd.shape[1:], BF16)],
        compiler_params=_params("arbitrary", "arbitrary"),
        name="ffn_up",
    )(xg, ssq, wg, wu, wd)


def _matmul_res_kernel(a_ref, w_ref, r_ref, *refs, scale, emit_norm):
    acc = jnp.dot(a_ref[...], w_ref[...].astype(BF16), preferred_element_type=F32)
    x_new = r_ref[...] + scale * acc
    if not emit_norm:
        (o_ref,) = refs
        o_ref[...] = x_new
        return
    g_ref, o_ref, xg_ref, ssq_ref = refs
    o_ref[...] = x_new
    xg_ref[...] = (x_new * g_ref[...]).astype(xg_ref.dtype)
    part = jnp.broadcast_to(jnp.sum(x_new * x_new, axis=-1, keepdims=True), ssq_ref.shape)

    @pl.when(pl.program_id(1) == 0)
    def _():
        ssq_ref[...] = part

    @pl.when(pl.program_id(1) > 0)
    def _():
        ssq_ref[...] += part


def _matmul_res(a, w_spec, w, res, scale, tm, tn, next_gain=None):
    m, k = a.shape
    n = res.shape[1]
    emit_norm = next_gain is not None
    in_specs = [pl.BlockSpec((tm, k), lambda i, j: (i, 0)),
                w_spec,
                pl.BlockSpec((tm, tn), lambda i, j: (i, j))]
    args = [a, w, res]
    out_specs = [pl.BlockSpec((tm, tn), lambda i, j: (i, j))]
    out_shape = [jax.ShapeDtypeStruct((m, n), F32)]
    if emit_norm:
        in_specs.append(pl.BlockSpec((1, tn), lambda i, j: (0, j)))
        args.append(next_gain.reshape(1, n))
        out_specs += [pl.BlockSpec((tm, tn), lambda i, j: (i, j)),
                      pl.BlockSpec((tm, LANES), lambda i, j: (i, 0))]
        out_shape += [jax.ShapeDtypeStruct((m, n), BF16), jax.ShapeDtypeStruct((m, LANES), F32)]
    out = pl.pallas_call(
        functools.partial(_matmul_res_kernel, scale=scale, emit_norm=emit_norm),
        grid=(m // tm, n // tn),
        in_specs=in_specs,
        out_specs=out_specs,
        out_shape=out_shape,
        compiler_params=_params("parallel", "arbitrary"),
        name="matmul_res",
    )(*args)
    return out if emit_norm else out[0]


def _mlstm_kernel(q_ref, k_ref, v_ref, o_ref, g_ref, gb_ref, gh_ref, c0_ref, n0_ref, m0_ref,
                  y_ref, c_out_ref, n_out_ref, m_out_ref, c_scr, n_scr, m_scr, *, chunk, n_chunks):
    c_idx = pl.program_id(1)
    L = chunk

    @pl.when(c_idx == 0)
    def _():
        c_scr[...] = c0_ref[0]
        n_scr[...] = n0_ref[0]
        m_scr[...] = m0_ref[0]

    gates = g_ref[...] + gb_ref[...]
    lane = lax.broadcasted_iota(jnp.int32, gates.shape, 1)
    row = lax.broadcasted_iota(jnp.int32, (L, L), 0)
    col = lax.broadcasted_iota(jnp.int32, (L, L), 1)
    eye = row == col
    causal = col <= row
    anti = row <= col

    for head in range(N_HEADS_A):
        qs = slice(head * DK_A, (head + 1) * DK_A)
        vs = slice(head * DV_A, (head + 1) * DV_A)
        i_col = jnp.sum(jnp.where(lane == head, gates, 0.0), axis=1, keepdims=True)
        f_col = jnp.sum(jnp.where(lane == head + N_HEADS_A, gates, 0.0), axis=1, keepdims=True)
        lf_col = jax.nn.log_sigmoid(f_col)
        lf_row = jnp.sum(jnp.where(eye, lf_col, 0.0), axis=0, keepdims=True)
        i_row = jnp.sum(jnp.where(eye, i_col, 0.0), axis=0, keepdims=True)
        b_col = jnp.sum(jnp.where(causal, lf_row, 0.0), axis=1, keepdims=True)
        b_row = jnp.sum(jnp.where(anti, lf_col, 0.0), axis=0, keepdims=True)

        m_prev = m_scr[head]
        dmat = jnp.where(causal, b_col - b_row + i_row, -jnp.inf)
        inter = b_col + m_prev
        mt = jnp.maximum(inter, jnp.max(dmat, axis=1, keepdims=True))

        q32 = q_ref[:, qs]
        k32 = k_ref[:, qs] * (DK_A ** -0.5)
        q = q32.astype(BF16)
        k = k32.astype(BF16)
        v = v_ref[:, vs].astype(BF16)
        c_prev = c_scr[head]
        n_prev = n_scr[head]

        s = lax.dot_general(q, k, _NT, preferred_element_type=F32)
        w = s * jnp.exp(dmat - mt)
        a = jnp.exp(inter - mt)
        num = (jnp.dot(w.astype(BF16), v, preferred_element_type=F32)
               + a * jnp.dot(q, c_prev.astype(BF16), preferred_element_type=F32))
        den = (jnp.sum(w, axis=1, keepdims=True)
               + a * jnp.sum(q32 * n_prev, axis=1, keepdims=True))
        h = num / jnp.maximum(jnp.abs(den), jnp.exp(-mt))

        m_new = mt[L - 1:L, :]
        b_last = b_col[L - 1:L, :]
        g = jnp.exp(b_last - b_col + i_col - m_new)
        dec = jnp.exp(b_last + m_prev - m_new)
        gk = g * k32
        c_new = dec * c_prev + lax.dot_general(gk.astype(BF16), v, (((0,), (0,)), ((), ())),
                                               preferred_element_type=F32)
        n_new = dec * n_prev + jnp.sum(gk, axis=0, keepdims=True)
        c_scr[head] = c_new
        n_scr[head] = n_new
        m_scr[head] = m_new

        hn = h * lax.rsqrt(jnp.mean(h * h, axis=-1, keepdims=True) + EPS)
        y_ref[:, vs] = (hn * gh_ref[:, vs] * jax.nn.sigmoid(o_ref[:, vs])).astype(y_ref.dtype)

    @pl.when(c_idx == n_chunks - 1)
    def _():
        c_out_ref[0] = c_scr[...]
        n_out_ref[0] = n_scr[...]
        m_out_ref[0] = m_scr[...]


def _mlstm(pa, gates, gate_bias, g_head, layer, c0, n0, m0, row_start, batch, seq, chunk, out_dtype):
    n_chunks = seq // chunk
    m_rows = batch * seq
    qk_w = N_HEADS_A * DK_A
    r0 = row_start // chunk
    h_a = N_HEADS_A

    def rows(b, c):
        return r0 + b * n_chunks + c

    def state_specs():
        return [pl.BlockSpec((1, h_a, DK_A, DV_A), lambda b, c: (b, 0, 0, 0)),
                pl.BlockSpec((1, h_a, 1, DK_A), lambda b, c: (b, 0, 0, 0)),
                pl.BlockSpec((1, h_a, 1, 1), lambda b, c: (b, 0, 0, 0))]

    kernel = functools.partial(_mlstm_kernel, chunk=chunk, n_chunks=n_chunks)
    y, c_fin, n_fin, m_fin = pl.pallas_call(
        kernel,
        grid=(batch, n_chunks),
        in_specs=[
            pl.BlockSpec((chunk, qk_w), lambda b, c: (rows(b, c), 0)),
            pl.BlockSpec((chunk, qk_w), lambda b, c: (rows(b, c), 1)),
            pl.BlockSpec((chunk, MIX_A), lambda b, c: (rows(b, c), 2 * qk_w // MIX_A)),
            pl.BlockSpec((chunk, MIX_A), lambda b, c: (rows(b, c), 2 * qk_w // MIX_A + 1)),
            pl.BlockSpec((chunk, LANES), lambda b, c: (rows(b, c), 0)),
            pl.BlockSpec((1, LANES), lambda b, c: (0, 0)),
            pl.BlockSpec((None, 1, MIX_A), lambda b, c: (layer, 0, 0)),
        ] + state_specs(),
        out_specs=[pl.BlockSpec((chunk, MIX_A), lambda b, c: (b * n_chunks + c, 0))] + state_specs(),
        out_shape=[
            jax.ShapeDtypeStruct((m_rows, MIX_A), out_dtype),
            jax.ShapeDtypeStruct((batch, h_a, DK_A, DV_A), F32),
            jax.ShapeDtypeStruct((batch, h_a, 1, DK_A), F32),
            jax.ShapeDtypeStruct((batch, h_a, 1, 1), F32),
        ],
        scratch_shapes=[pltpu.VMEM((h_a, DK_A, DV_A), F32),
                        pltpu.VMEM((h_a, 1, DK_A), F32),
                        pltpu.VMEM((h_a, 1, 1), F32)],
        compiler_params=_params("parallel", "arbitrary"),
        name="mlstm",
    )(pa, pa, pa, pa, gates, gate_bias, g_head.reshape(g_head.shape[0], 1, MIX_A),
      c0, n0.reshape(batch, h_a, 1, DK_A), m0.reshape(batch, h_a, 1, 1))
    return (y, c_fin, n_fin.reshape(batch, h_a, DK_A), m_fin.reshape(batch, h_a))


def _t5_bucket(dist):
    max_exact = N_BUCKETS // 2
    d = np.maximum(dist, 1).astype(np.float64)
    large = max_exact + (np.log(d / max_exact) / math.log(MAX_DISTANCE / max_exact)
                         * (N_BUCKETS - max_exact)).astype(np.int32)
    large = np.minimum(large, N_BUCKETS - 1)
    return np.where(dist < max_exact, dist, large).astype(np.int32)


def _prompt_bias_rows(rel_bias):
    j = np.arange(2 * N_STEPS)
    valid = j <= N_STEPS
    rows = []
    for (_, r) in PATTERNS:
        bucket = _t5_bucket(np.clip(N_STEPS - j, 0, N_STEPS) * r)
        rows.append(jnp.where(valid[:, None], rel_bias[bucket].astype(F32), NEG_INF))
    return jnp.stack(rows, 0).transpose(2, 0, 1)[:, :, None, :]


def _attn_prompt_kernel(q_ref, k_ref, v_ref, bias_ref, y_ref, acc_scr, m_scr, l_scr, *, seq):
    scale = HEAD_DIM ** -0.5
    nw = N_STEPS
    n_blocks = seq // nw

    for p, (_, r) in enumerate(PATTERNS):
        nb = n_blocks // r

        def rows(g):
            c, n = divmod(g, nb)
            start = c + n * nw * r
            return pl.ds(start, nw, stride=r) if r > 1 else pl.ds(start, nw)

        tile = pltpu.roll(jnp.broadcast_to(bias_ref[0, p], (nw, 2 * nw)), 0, 1,
                          stride=1, stride_axis=0)
        tile_cur = tile[:, nw:]
        qs = [q_ref[rows(g), :].astype(BF16) for g in range(n_blocks)]
        ks = [k_ref[rows(g), :].astype(BF16) for g in range(n_blocks)]
        vs = [v_ref[rows(g), :].astype(BF16) for g in range(n_blocks)]
        for g in range(n_blocks):
            if g % nb:
                kk = jnp.concatenate([ks[g - 1], ks[g]], axis=0)
                vv = jnp.concatenate([vs[g - 1], vs[g]], axis=0)
                bias = tile
            else:
                kk, vv, bias = ks[g], vs[g], tile_cur
            s = lax.dot_general(qs[g], kk, _NT, preferred_element_type=F32)
            s = s * scale + bias
            m = jnp.max(s, axis=1, keepdims=True)
            e = jnp.exp(s - m)
            l = jnp.sum(e, axis=1, keepdims=True)
            acc = jnp.dot(e.astype(BF16), vv, preferred_element_type=F32)
            acc_scr[p, rows(g), :] = acc
            m_scr[p, rows(g), :] = jnp.broadcast_to(m, (nw, HEAD_DIM))
            l_scr[p, rows(g), :] = jnp.broadcast_to(l, (nw, HEAD_DIM))

    rows_per_step = 256

    def merge(t, carry):
        rws = pl.ds(pl.multiple_of(t * rows_per_step, rows_per_step), rows_per_step)
        m0, m1, m2 = m_scr[0, rws, :], m_scr[1, rws, :], m_scr[2, rws, :]
        mx = jnp.maximum(jnp.maximum(m0, m1), m2)
        w0, w1, w2 = jnp.exp(m0 - mx), jnp.exp(m1 - mx), jnp.exp(m2 - mx)
        num = w0 * acc_scr[0, rws, :] + w1 * acc_scr[1, rws, :] + w2 * acc_scr[2, rws, :]
        den = w0 * l_scr[0, rws, :] + w1 * l_scr[1, rws, :] + w2 * l_scr[2, rws, :]
        y_ref[rws, :] = (num / den).astype(y_ref.dtype)
        return carry
    lax.fori_loop(0, seq // rows_per_step, merge, 0)


def _attn_prompt(q, k_all, v_all, layer, bias_rows, batch, seq):
    n_pat = len(PATTERNS)
    kernel = functools.partial(_attn_prompt_kernel, seq=seq)
    return pl.pallas_call(
        kernel,
        grid=(batch, N_HEADS_B),
        in_specs=[
            pl.BlockSpec((seq, HEAD_DIM), lambda b, h: (b, h)),
            pl.BlockSpec((None, seq, HEAD_DIM), lambda b, h: (layer, b, h)),
            pl.BlockSpec((None, seq, HEAD_DIM), lambda b, h: (layer, b, h)),
            pl.BlockSpec((1, n_pat, 1, 2 * N_STEPS), lambda b, h: (h, 0, 0, 0)),
        ],
        out_specs=pl.BlockSpec((seq, HEAD_DIM), lambda b, h: (b, h)),
        out_shape=jax.ShapeDtypeStruct((batch * seq, MIX_B), BF16),
        scratch_shapes=[pltpu.VMEM((n_pat, seq, HEAD_DIM), F32),
                        pltpu.VMEM((n_pat, seq, HEAD_DIM), F32),
                        pltpu.VMEM((n_pat, seq, HEAD_DIM), F32)],
        compiler_params=_params("parallel", "parallel"),
        name="attn_prompt",
    )(q, k_all, v_all, bias_rows)


def _sample_bias(rel_bias, wbuf, t_new):
    n = wbuf + t_new
    delta = np.arange(n)
    rows = []
    for (w, r) in PATTERNS:
        valid = (delta % r == 0) & (delta // r <= w // r)
        b = rel_bias[_t5_bucket(np.minimum(delta, w))].astype(F32)
        rows.append(jnp.where(valid[:, None], b, NEG_INF))
    f = jnp.stack(rows, 0).transpose(2, 0, 1)
    f_rev = jnp.pad(f[:, :, ::-1], ((0, 0), (0, 0), (0, t_new)), constant_values=NEG_INF)
    bias_c = jnp.stack([f_rev[:, :, t_new - 1 - t: t_new - 1 - t + wbuf] for t in range(t_new)], axis=2)
    bias_n = jnp.stack([f_rev[:, :, n - 1 - t: n - 1 - t + t_new] for t in range(t_new)], axis=2)
    return bias_c, bias_n


def _attn_sample_kernel(q_ref, kn_ref, vn_ref, kc_ref, vc_ref, kx_ref, vx_ref, bc_ref, bn_ref, *refs,
                        n_chunks):
    y_ref, ko_ref, vo_ref, m_scr, l_scr, acc_scr = refs[-6:]
    scale = HEAD_DIM ** -0.5
    n_pat = len(PATTERNS)
    chunk = pl.program_id(1)
    t_new = q_ref.shape[0]
    rows = kc_ref.shape[0]
    shift = t_new * N_HEADS_B
    tokens = rows // N_HEADS_B

    ko_ref[0:rows - shift, :] = kc_ref[shift:rows, :]
    vo_ref[0:rows - shift, :] = vc_ref[shift:rows, :]

    @pl.when(chunk < n_chunks - 1)
    def _():
        ko_ref[rows - shift:rows, :] = kx_ref[...]
        vo_ref[rows - shift:rows, :] = vx_ref[...]

    @pl.when(chunk == n_chunks - 1)
    def _():
        for h in range(N_HEADS_B):
            dst = pl.ds(rows - shift + h, t_new, stride=N_HEADS_B)
            ko_ref[dst, :] = kn_ref[:, h * HEAD_DIM:(h + 1) * HEAD_DIM]
            vo_ref[dst, :] = vn_ref[:, h * HEAD_DIM:(h + 1) * HEAD_DIM]

    @pl.when(chunk == 0)
    def _():
        m_scr[...] = jnp.full(m_scr.shape, NEG_INF, F32)
        l_scr[...] = jnp.zeros(l_scr.shape, F32)
        acc_scr[...] = jnp.zeros(acc_scr.shape, F32)

    def update(h, q, k, v, biases):
        s = lax.dot_general(q, k, _NT, preferred_element_type=F32) * scale
        zs = [s + b for b in biases]
        zmax = jnp.max(zs[0], axis=1, keepdims=True)
        for z in zs[1:]:
            zmax = jnp.maximum(zmax, jnp.max(z, axis=1, keepdims=True))
        m_old = m_scr[h]
        m_new = jnp.maximum(m_old, zmax)
        alpha = jnp.exp(m_old - m_new)
        pt = jnp.exp(zs[0] - m_new)
        for z in zs[1:]:
            pt = pt + jnp.exp(z - m_new)
        m_scr[h] = m_new
        l_scr[h] = alpha * l_scr[h] + jnp.sum(pt, axis=1, keepdims=True)
        acc_scr[h] = alpha * acc_scr[h] + jnp.dot(pt.astype(BF16), v, preferred_element_type=F32)

    for h in range(N_HEADS_B):
        head_rows = pl.ds(h, tokens, stride=N_HEADS_B)
        q = q_ref[:, h * HEAD_DIM:(h + 1) * HEAD_DIM].astype(BF16)
        update(h, q, kc_ref[head_rows, :].astype(BF16), vc_ref[head_rows, :].astype(BF16),
               [bc_ref[h, p] for p in range(n_pat)])

    @pl.when(chunk == n_chunks - 1)
    def _():
        for h in range(N_HEADS_B):
            cols = slice(h * HEAD_DIM, (h + 1) * HEAD_DIM)
            q = q_ref[:, cols].astype(BF16)
            update(h, q, kn_ref[:, cols].astype(BF16), vn_ref[:, cols].astype(BF16),
                   [bn_ref[h, p] for p in range(n_pat)])
            y_ref[:, cols] = (acc_scr[h] / l_scr[h]).astype(y_ref.dtype)


def _attn_sample(q, kn, vn, cache_k, cache_v, layer, bias_c, bias_n, batch, t_new, prev_k, prev_v,
                 n_chunks=4):
    depth, _, all_rows, _ = cache_k.shape
    rows = all_rows // n_chunks
    shift = t_new * N_HEADS_B
    n_pat = len(PATTERNS)
    chunk_spec = pl.BlockSpec((None, None, rows, HEAD_DIM), lambda b, c: (layer, b, c, 0))
    next_spec = pl.BlockSpec((None, None, shift, HEAD_DIM),
                             lambda b, c: (layer, b, jnp.minimum(c + 1, n_chunks - 1) * (rows // shift), 0))
    row_spec = pl.BlockSpec((t_new, MIX_B), lambda b, c: (b, 0))
    in_specs = [row_spec, row_spec, row_spec, chunk_spec, chunk_spec, next_spec, next_spec,
                pl.BlockSpec((N_HEADS_B, n_pat, t_new, rows // N_HEADS_B), lambda b, c: (0, 0, 0, c)),
                pl.BlockSpec((N_HEADS_B, n_pat, t_new, t_new), lambda b, c: (0, 0, 0, 0))]
    args = [q, kn, vn, cache_k, cache_v, cache_k, cache_v, bias_c, bias_n]
    aliases = {}
    if prev_k is not None:
        in_specs += [pl.BlockSpec(memory_space=pl.ANY), pl.BlockSpec(memory_space=pl.ANY)]
        args += [prev_k, prev_v]
        aliases = {9: 1, 10: 2}
    cache_shape = jax.ShapeDtypeStruct(cache_k.shape, F32)
    return pl.pallas_call(
        functools.partial(_attn_sample_kernel, n_chunks=n_chunks),
        grid=(batch, n_chunks),
        in_specs=in_specs,
        out_specs=[row_spec, chunk_spec, chunk_spec],
        out_shape=[jax.ShapeDtypeStruct((batch * t_new, MIX_B), F32), cache_shape, cache_shape],
        scratch_shapes=[pltpu.VMEM((N_HEADS_B, t_new, 1), F32),
                        pltpu.VMEM((N_HEADS_B, t_new, 1), F32),
                        pltpu.VMEM((N_HEADS_B, t_new, HEAD_DIM), F32)],
        input_output_aliases=aliases,
        compiler_params=_params("parallel", "arbitrary"),
        name="attn_sample",
    )(*args)


def kernel(x_prompt, x_sample, state_mlstm_C, state_mlstm_n, state_mlstm_m, cache_win_k, cache_win_v,
           g_ffn1, w_ffn1_gate, w_ffn1_up, w_ffn1_down, g_mix, w_in, b_igate, b_fgate, g_head, w_out,
           rel_bias, g_ffn2, w_ffn2_gate, w_ffn2_up, w_ffn2_down, g_final):
    bp, sp, d = x_prompt.shape
    bs, ts, _ = x_sample.shape
    depth = g_mix.shape[0]
    wbuf = cache_win_k.shape[2]
    mp, ms = bp * sp, bs * ts
    m_all = mp + ms
    d_ff = w_ffn1_gate.shape[2]
    tm_big = _row_tile(m_all, (1376, 1024, 512, 64))
    tm_down = _row_tile(m_all, (688, 512, 64))

    x = jnp.concatenate([x_prompt.reshape(mp, d), x_sample.reshape(ms, d)], axis=0)
    zc = jnp.zeros((bp, N_HEADS_A, DK_A, DV_A), F32)
    zn = jnp.zeros((bp, N_HEADS_A, DK_A), F32)
    zm = jnp.zeros((bp, N_HEADS_A), F32)
    bias_rows = _prompt_bias_rows(rel_bias)
    bias_c, bias_n = _sample_bias(rel_bias, wbuf, ts)
    cache_k = cache_win_k.reshape(depth, bs, wbuf * N_HEADS_B, HEAD_DIM)
    cache_v = cache_win_v.reshape(depth, bs, wbuf * N_HEADS_B, HEAD_DIM)
    w_in_t = jnp.swapaxes(w_in, 1, 2)

    def ffn(x, xg, ssq, wg, wu, wd, l, next_gain):
        a, wd16 = _ffn_up(xg, ssq, wg, wu, wd, l, tm_big)
        w_spec = pl.BlockSpec((d_ff, 256), lambda i, j: (0, j))
        return _matmul_res(a, w_spec, wd16, x, 0.5, tm_down, 256, next_gain)

    p_c, p_n, p_m, s_c, s_n, s_m = [], [], [], [], [], []
    pk = pv = sk = sv = None
    xg, ssq = _gain_ssq(x, g_ffn1[0])
    for l in range(depth):
        x, xg, ssq = ffn(x, xg, ssq, w_ffn1_gate, w_ffn1_up, w_ffn1_down, l, g_mix[l])

        pa = _proj_nt(xg, ssq, w_in_t, l, 0, A_WIDTH, 256, tm_big)
        gates = _proj_nt(xg, ssq, w_in_t, l, A_WIDTH // LANES, LANES, LANES, tm_big)
        gate_bias = jnp.pad(jnp.concatenate([b_igate[l], b_fgate[l]]),
                            (0, LANES - GATE_COLS)).reshape(1, LANES)

        ya_p, c_, n_, m_ = _mlstm(pa, gates, gate_bias, g_head, l, zc, zn, zm,
                                  0, bp, sp, PROMPT_CHUNK, BF16)
        p_c.append(c_); p_n.append(n_); p_m.append(m_)
        ya_s, c_, n_, m_ = _mlstm(pa, gates, gate_bias, g_head, l, state_mlstm_C[l], state_mlstm_n[l],
                                  state_mlstm_m[l], mp, bs, ts, ts, F32)
        s_c.append(c_); s_n.append(n_); s_m.append(m_)

        q_p, pk, pv = _qkv_prompt(xg, ssq, w_in_t, l, depth, mp, pk, pv)
        yb_p = _attn_prompt(q_p, pk, pv, l, bias_rows, bp, sp)
        q_s, k_s, v_s = _qkv_sample(xg, ssq, w_in_t, l, mp, ms)
        yb_s, sk, sv = _attn_sample(q_s, k_s, v_s, cache_k, cache_v, l, bias_c, bias_n, bs, ts, sk, sv)

        y = jnp.concatenate([jnp.concatenate([ya_p, yb_p], axis=1),
                             jnp.concatenate([ya_s, yb_s], axis=1).astype(BF16)], axis=0)
        wo_spec = pl.BlockSpec((None, d, 256), lambda i, j, l=l: (l, 0, j))
        x, xg, ssq = _matmul_res(y, wo_spec, w_out, x, 1.0, tm_big, 256, g_ffn2[l])

        if l + 1 < depth:
            x, xg, ssq = ffn(x, xg, ssq, w_ffn2_gate, w_ffn2_up, w_ffn2_down, l, g_ffn1[l + 1])
        else:
            x = ffn(x, xg, ssq, w_ffn2_gate, w_ffn2_up, w_ffn2_down, l, None)

    yp = _rmsnorm(x, g_final, F32, 0, mp).reshape(bp, sp, d)
    ys = _rmsnorm(x, g_final, F32, mp, ms).reshape(bs, ts, d)
    return (yp, ys, jnp.stack(p_c), jnp.stack(p_n), jnp.stack(p_m),
            pk.reshape(depth, bp, sp, N_HEADS_B, HEAD_DIM), pv.reshape(depth, bp, sp, N_HEADS_B, HEAD_DIM),
            jnp.stack(s_c), jnp.stack(s_n), jnp.stack(s_m),
            sk.reshape(depth, bs, wbuf, N_HEADS_B, HEAD_DIM), sv.reshape(depth, bs, wbuf, N_HEADS_B, HEAD_DIM))
```

```python
import functools
import math

import numpy as np
import jax
import jax.numpy as jnp
from jax import lax
from jax.experimental import pallas as pl
from jax.experimental.pallas import tpu as pltpu

F32 = jnp.float32
BF16 = jnp.bfloat16

D_MODEL = 4096
N_HEADS_A = 8
DK_A = 128
DV_A = 256
MIX_A = N_HEADS_A * DV_A
N_HEADS_B = 16
HEAD_DIM = 128
MIX_B = N_HEADS_B * HEAD_DIM
PATTERNS = ((128, 1), (512, 4), (2048, 16))
N_STEPS = 128
N_BUCKETS = 32
MAX_DISTANCE = 2048
EPS = 1e-6
NEG_INF = -1e30
A_WIDTH = 2 * N_HEADS_A * DK_A + 2 * MIX_A
GATE_COLS = 2 * N_HEADS_A
LANES = 128
SUBLANES = 8
PROMPT_CHUNK = 512

VMEM_LIMIT = 56 * 1024 * 1024


def _params(*sem):
    return pltpu.CompilerParams(dimension_semantics=sem, vmem_limit_bytes=VMEM_LIMIT)


def _row_tile(m, candidates):
    for t in candidates:
        if m % t == 0:
            return t
    raise ValueError(f"no row tile for {m} rows among {candidates}")


def _inv_rms(ssq_ref, d):
    return lax.rsqrt(ssq_ref[:, 0:1] * (1.0 / d) + EPS)


def _rmsnorm_kernel(x_ref, g_ref, o_ref):
    x = x_ref[...]
    ms = jnp.mean(x * x, axis=-1, keepdims=True)
    o_ref[...] = (x * lax.rsqrt(ms + EPS) * g_ref[...]).astype(o_ref.dtype)


def _rmsnorm(x, g, out_dtype, row_start, n_rows):
    _, d = x.shape
    tm = _row_tile(math.gcd(n_rows, row_start) if row_start else n_rows, (256, 192, 128, 64))
    off = row_start // tm
    return pl.pallas_call(
        _rmsnorm_kernel,
        grid=(n_rows // tm,),
        in_specs=[pl.BlockSpec((tm, d), lambda i: (off + i, 0)),
                  pl.BlockSpec((1, d), lambda i: (0, 0))],
        out_specs=pl.BlockSpec((tm, d), lambda i: (i, 0)),
        out_shape=jax.ShapeDtypeStruct((n_rows, d), out_dtype),
        compiler_params=_params("parallel"),
        name="rmsnorm",
    )(x, g.reshape(1, d))


def _gain_ssq_kernel(x_ref, g_ref, xg_ref, ssq_ref):
    x = x_ref[...]
    xg_ref[...] = (x * g_ref[...]).astype(xg_ref.dtype)
    ssq_ref[...] = jnp.broadcast_to(jnp.sum(x * x, axis=-1, keepdims=True), ssq_ref.shape)


def _gain_ssq(x, g):
    m, d = x.shape
    tm = _row_tile(m, (256, 192, 128, 64))
    return pl.pallas_call(
        _gain_ssq_kernel,
        grid=(m // tm,),
        in_specs=[pl.BlockSpec((tm, d), lambda i: (i, 0)),
                  pl.BlockSpec((1, d), lambda i: (0, 0))],
        out_specs=[pl.BlockSpec((tm, d), lambda i: (i, 0)),
                   pl.BlockSpec((tm, LANES), lambda i: (i, 0))],
        out_shape=[jax.ShapeDtypeStruct((m, d), BF16), jax.ShapeDtypeStruct((m, LANES), F32)],
        compiler_params=_params("parallel"),
        name="gain_ssq",
    )(x, g.reshape(1, d))


_NT = (((1,), (1,)), ((), ()))


def _row_chunks(ref, n):
    rows = ref.shape[0] // n
    return [slice(c * rows, (c + 1) * rows) for c in range(n)]


def _proj_nt_kernel(x_ref, ssq_ref, w_ref, o_ref, *, row_chunks):
    w = w_ref[...].astype(BF16)
    for rs in _row_chunks(x_ref, row_chunks):
        acc = lax.dot_general(x_ref[rs, :], w, _NT, preferred_element_type=F32)
        o_ref[rs, :] = acc * lax.rsqrt(ssq_ref[rs, 0:1] * (1.0 / x_ref.shape[1]) + EPS)


def _proj_nt(xg, ssq, wt, layer, row_block0, n_cols, tn, tm):
    m, k = xg.shape
    return pl.pallas_call(
        functools.partial(_proj_nt_kernel, row_chunks=2 if tm % 32 == 0 else 1),
        grid=(m // tm, n_cols // tn),
        in_specs=[pl.BlockSpec((tm, k), lambda i, j: (i, 0)),
                  pl.BlockSpec((tm, LANES), lambda i, j: (i, 0)),
                  pl.BlockSpec((None, tn, k), lambda i, j: (layer, row_block0 + j, 0))],
        out_specs=pl.BlockSpec((tm, tn), lambda i, j: (i, j)),
        out_shape=jax.ShapeDtypeStruct((m, n_cols), F32),
        compiler_params=_params("parallel", "arbitrary"),
        name="proj_nt",
    )(xg, ssq, wt)


def _qkv_kernel(x_ref, ssq_ref, wq_ref, wk_ref, wv_ref, *refs):
    q_ref, k_ref, v_ref = refs[-3:]
    x = x_ref[...]
    inv = _inv_rms(ssq_ref, x_ref.shape[1])
    for w_ref, o_ref in ((wq_ref, q_ref), (wk_ref, k_ref), (wv_ref, v_ref)):
        acc = lax.dot_general(x, w_ref[...].astype(BF16), _NT, preferred_element_type=F32)
        o_ref[...] = acc * inv


def _qkv_weight_specs(wt2d_rows, layer, tn, k_dim):
    base = layer * wt2d_rows + A_WIDTH + GATE_COLS
    specs = []
    for part in range(3):
        def index(*ids, part=part):
            j = ids[-1]
            return (pl.multiple_of(base + part * MIX_B + j * tn, SUBLANES), 0)
        specs.append(pl.BlockSpec((pl.Element(tn), pl.Element(k_dim)), index))
    return specs


def _qkv_prompt(xg, ssq, wt, layer, depth, n_rows, prev_k, prev_v, tm=1024, tn=256):
    k_dim = xg.shape[1]
    nb = MIX_B // tn
    n_all = wt.shape[1]
    kv_shape = jax.ShapeDtypeStruct((depth, n_rows, MIX_B), F32)
    in_specs = ([pl.BlockSpec((tm, k_dim), lambda i, j: (i, 0)),
                 pl.BlockSpec((tm, LANES), lambda i, j: (i, 0))]
                + _qkv_weight_specs(n_all, layer, tn, k_dim))
    wt2d = wt.reshape(depth * n_all, k_dim)
    args = [xg, ssq, wt2d, wt2d, wt2d]
    aliases = {}
    if prev_k is not None:
        in_specs += [pl.BlockSpec(memory_space=pl.ANY), pl.BlockSpec(memory_space=pl.ANY)]
        args += [prev_k, prev_v]
        aliases = {5: 1, 6: 2}
    return pl.pallas_call(
        _qkv_kernel,
        grid=(n_rows // tm, nb),
        in_specs=in_specs,
        out_specs=[pl.BlockSpec((tm, tn), lambda i, j: (i, j)),
                   pl.BlockSpec((None, tm, tn), lambda i, j: (layer, i, j)),
                   pl.BlockSpec((None, tm, tn), lambda i, j: (layer, i, j))],
        out_shape=[jax.ShapeDtypeStruct((n_rows, MIX_B), F32), kv_shape, kv_shape],
        input_output_aliases=aliases,
        compiler_params=_params("parallel", "arbitrary"),
        name="qkv_prompt",
    )(*args)


def _qkv_sample(xg, ssq, wt, layer, row_start, n_rows, tn=256):
    k_dim = xg.shape[1]
    nb = MIX_B // tn
    depth, n_all, _ = wt.shape
    off = row_start // n_rows
    out = jax.ShapeDtypeStruct((n_rows, MIX_B), F32)
    wt2d = wt.reshape(depth * n_all, k_dim)
    return pl.pallas_call(
        _qkv_kernel,
        grid=(nb,),
        in_specs=([pl.BlockSpec((n_rows, k_dim), lambda j: (off, 0)),
                   pl.BlockSpec((n_rows, LANES), lambda j: (off, 0))]
                  + _qkv_weight_specs(n_all, layer, tn, k_dim)),
        out_specs=[pl.BlockSpec((n_rows, tn), lambda j: (0, j))] * 3,
        out_shape=[out, out, out],
        compiler_params=_params("arbitrary"),
        name="qkv_sample",
    )(xg, ssq, wt2d, wt2d, wt2d)


def _ffn_up_kernel(x_ref, ssq_ref, wg_ref, wu_ref, wd_ref, o_ref, wd16_ref, *, row_chunks):
    wg = wg_ref[...].astype(BF16)
    wu = wu_ref[...].astype(BF16)
    for rs in _row_chunks(x_ref, row_chunks):
        x = x_ref[rs, :]
        inv = lax.rsqrt(ssq_ref[rs, 0:1] * (1.0 / x_ref.shape[1]) + EPS)
        g = jnp.dot(x, wg, preferred_element_type=F32) * inv
        u = jnp.dot(x, wu, preferred_element_type=F32) * inv
        o_ref[rs, :] = (g * jax.nn.sigmoid(g) * u).astype(o_ref.dtype)
    wd16_ref[...] = wd_ref[...].astype(wd16_ref.dtype)


def _ffn_up(xg, ssq, wg, wu, wd, layer, tm, tn=256, wd_rows=64, row_chunks=2):
    m, k = xg.shape
    n = wg.shape[2]
    n_j = n // tn
    n_slabs = wd.shape[1] // wd_rows
    assert wd.shape[1] % wd_rows == 0 and n_slabs <= (m // tm) * n_j

    def slab(i, j):
        return jnp.minimum(i * n_j + j, n_slabs - 1)

    return pl.pallas_call(
        functools.partial(_ffn_up_kernel, row_chunks=row_chunks),
        grid=(m // tm, n_j),
        in_specs=[pl.BlockSpec((tm, k), lambda i, j: (i, 0)),
                  pl.BlockSpec((tm, LANES), lambda i, j: (i, 0)),
                  pl.BlockSpec((None, k, tn), lambda i, j: (layer, 0, j)),
                  pl.BlockSpec((None, k, tn), lambda i, j: (layer, 0, j)),
                  pl.BlockSpec((None, wd_rows, wd.shape[2]), lambda i, j: (layer, slab(i, j), 0))],
        out_specs=[pl.BlockSpec((tm, tn), lambda i, j: (i, j)),
                   pl.BlockSpec((wd_rows, wd.shape[2]), lambda i, j: (slab(i, j), 0))],
        out_shape=[jax.ShapeDtypeStruct((m, n), BF16),
                   jax.ShapeDtypeStruct(wd.shape[1:], BF16)],
        compiler_params=_params("arbitrary", "arbitrary"),
        name="ffn_up",
    )(xg, ssq, wg, wu, wd)


def _matmul_res_kernel(a_ref, w_ref, r_ref, *refs, scale, emit_norm):
    acc = jnp.dot(a_ref[...], w_ref[...].astype(BF16), preferred_element_type=F32)
    x_new = r_ref[...] + scale * acc
    if not emit_norm:
        (o_ref,) = refs
        o_ref[...] = x_new
        return
    g_ref, o_ref, xg_ref, ssq_ref = refs
    o_ref[...] = x_new
    xg_ref[...] = (x_new * g_ref[...]).astype(xg_ref.dtype)
    part = jnp.broadcast_to(jnp.sum(x_new * x_new, axis=-1, keepdims=True), ssq_ref.shape)

    @pl.when(pl.program_id(1) == 0)
    def _():
        ssq_ref[...] = part

    @pl.when(pl.program_id(1) > 0)
    def _():
        ssq_ref[...] += part


def _matmul_res(a, w_spec, w, res, scale, tm, tn, next_gain=None):
    m, k = a.shape
    n = res.shape[1]
    emit_norm = next_gain is not None
    in_specs = [pl.BlockSpec((tm, k), lambda i, j: (i, 0)),
                w_spec,
                pl.BlockSpec((tm, tn), lambda i, j: (i, j))]
    args = [a, w, res]
    out_specs = [pl.BlockSpec((tm, tn), lambda i, j: (i, j))]
    out_shape = [jax.ShapeDtypeStruct((m, n), F32)]
    if emit_norm:
        in_specs.append(pl.BlockSpec((1, tn), lambda i, j: (0, j)))
        args.append(next_gain.reshape(1, n))
        out_specs += [pl.BlockSpec((tm, tn), lambda i, j: (i, j)),
                      pl.BlockSpec((tm, LANES), lambda i, j: (i, 0))]
        out_shape += [jax.ShapeDtypeStruct((m, n), BF16), jax.ShapeDtypeStruct((m, LANES), F32)]
    out = pl.pallas_call(
        functools.partial(_matmul_res_kernel, scale=scale, emit_norm=emit_norm),
        grid=(m // tm, n // tn),
        in_specs=in_specs,
        out_specs=out_specs,
        out_shape=out_shape,
        compiler_params=_params("parallel", "arbitrary"),
        name="matmul_res",
    )(*args)
    return out if emit_norm else out[0]


def _mlstm_kernel(q_ref, k_ref, v_ref, o_ref, g_ref, gb_ref, gh_ref, c0_ref, n0_ref, m0_ref,
                  y_ref, c_out_ref, n_out_ref, m_out_ref, c_scr, n_scr, m_scr, *, chunk, n_chunks):
    c_idx = pl.program_id(1)
    L = chunk

    @pl.when(c_idx == 0)
    def _():
        c_scr[...] = c0_ref[0]
        n_scr[...] = n0_ref[0]
        m_scr[...] = m0_ref[0]

    gates = g_ref[...] + gb_ref[...]
    lane = lax.broadcasted_iota(jnp.int32, gates.shape, 1)
    row = lax.broadcasted_iota(jnp.int32, (L, L), 0)
    col = lax.broadcasted_iota(jnp.int32, (L, L), 1)
    eye = row == col
    causal = col <= row
    anti = row <= col

    for head in range(N_HEADS_A):
        qs = slice(head * DK_A, (head + 1) * DK_A)
        vs = slice(head * DV_A, (head + 1) * DV_A)
        i_col = jnp.sum(jnp.where(lane == head, gates, 0.0), axis=1, keepdims=True)
        f_col = jnp.sum(jnp.where(lane == head + N_HEADS_A, gates, 0.0), axis=1, keepdims=True)
        lf_col = jax.nn.log_sigmoid(f_col)
        lf_row = jnp.sum(jnp.where(eye, lf_col, 0.0), axis=0, keepdims=True)
        i_row = jnp.sum(jnp.where(eye, i_col, 0.0), axis=0, keepdims=True)
        b_col = jnp.sum(jnp.where(causal, lf_row, 0.0), axis=1, keepdims=True)
        b_row = jnp.sum(jnp.where(anti, lf_col, 0.0), axis=0, keepdims=True)

        m_prev = m_scr[head]
        dmat = jnp.where(causal, b_col - b_row + i_row, -jnp.inf)
        inter = b_col + m_prev
        mt = jnp.maximum(inter, jnp.max(dmat, axis=1, keepdims=True))

        q32 = q_ref[:, qs]
        k32 = k_ref[:, qs] * (DK_A ** -0.5)
        q = q32.astype(BF16)
        k = k32.astype(BF16)
        v = v_ref[:, vs].astype(BF16)
        c_prev = c_scr[head]
        n_prev = n_scr[head]

        s = lax.dot_general(q, k, _NT, preferred_element_type=F32)
        w = s * jnp.exp(dmat - mt)
        a = jnp.exp(inter - mt)
        num = (jnp.dot(w.astype(BF16), v, preferred_element_type=F32)
               + a * jnp.dot(q, c_prev.astype(BF16), preferred_element_type=F32))
        den = (jnp.sum(w, axis=1, keepdims=True)
               + a * jnp.sum(q32 * n_prev, axis=1, keepdims=True))
        h = num / jnp.maximum(jnp.abs(den), jnp.exp(-mt))

        m_new = mt[L - 1:L, :]
        b_last = b_col[L - 1:L, :]
        g = jnp.exp(b_last - b_col + i_col - m_new)
        dec = jnp.exp(b_last + m_prev - m_new)
        gk = g * k32
        c_new = dec * c_prev + lax.dot_general(gk.astype(BF16), v, (((0,), (0,)), ((), ())),
                                               preferred_element_type=F32)
        n_new = dec * n_prev + jnp.sum(gk, axis=0, keepdims=True)
        c_scr[head] = c_new
        n_scr[head] = n_new
        m_scr[head] = m_new

        hn = h * lax.rsqrt(jnp.mean(h * h, axis=-1, keepdims=True) + EPS)
        y_ref[:, vs] = (hn * gh_ref[:, vs] * jax.nn.sigmoid(o_ref[:, vs])).astype(y_ref.dtype)

    @pl.when(c_idx == n_chunks - 1)
    def _():
        c_out_ref[0] = c_scr[...]
        n_out_ref[0] = n_scr[...]
        m_out_ref[0] = m_scr[...]


def _mlstm(pa, gates, gate_bias, g_head, layer, c0, n0, m0, row_start, batch, seq, chunk, out_dtype):
    n_chunks = seq // chunk
    m_rows = batch * seq
    qk_w = N_HEADS_A * DK_A
    r0 = row_start // chunk
    h_a = N_HEADS_A

    def rows(b, c):
        return r0 + b * n_chunks + c

    def state_specs():
        return [pl.BlockSpec((1, h_a, DK_A, DV_A), lambda b, c: (b, 0, 0, 0)),
                pl.BlockSpec((1, h_a, 1, DK_A), lambda b, c: (b, 0, 0, 0)),
                pl.BlockSpec((1, h_a, 1, 1), lambda b, c: (b, 0, 0, 0))]

    kernel = functools.partial(_mlstm_kernel, chunk=chunk, n_chunks=n_chunks)
    y, c_fin, n_fin, m_fin = pl.pallas_call(
        kernel,
        grid=(batch, n_chunks),
        in_specs=[
            pl.BlockSpec((chunk, qk_w), lambda b, c: (rows(b, c), 0)),
            pl.BlockSpec((chunk, qk_w), lambda b, c: (rows(b, c), 1)),
            pl.BlockSpec((chunk, MIX_A), lambda b, c: (rows(b, c), 2 * qk_w // MIX_A)),
            pl.BlockSpec((chunk, MIX_A), lambda b, c: (rows(b, c), 2 * qk_w // MIX_A + 1)),
            pl.BlockSpec((chunk, LANES), lambda b, c: (rows(b, c), 0)),
            pl.BlockSpec((1, LANES), lambda b, c: (0, 0)),
            pl.BlockSpec((None, 1, MIX_A), lambda b, c: (layer, 0, 0)),
        ] + state_specs(),
        out_specs=[pl.BlockSpec((chunk, MIX_A), lambda b, c: (b * n_chunks + c, 0))] + state_specs(),
        out_shape=[
            jax.ShapeDtypeStruct((m_rows, MIX_A), out_dtype),
            jax.ShapeDtypeStruct((batch, h_a, DK_A, DV_A), F32),
            jax.ShapeDtypeStruct((batch, h_a, 1, DK_A), F32),
            jax.ShapeDtypeStruct((batch, h_a, 1, 1), F32),
        ],
        scratch_shapes=[pltpu.VMEM((h_a, DK_A, DV_A), F32),
                        pltpu.VMEM((h_a, 1, DK_A), F32),
                        pltpu.VMEM((h_a, 1, 1), F32)],
        compiler_params=_params("parallel", "arbitrary"),
        name="mlstm",
    )(pa, pa, pa, pa, gates, gate_bias, g_head.reshape(g_head.shape[0], 1, MIX_A),
      c0, n0.reshape(batch, h_a, 1, DK_A), m0.reshape(batch, h_a, 1, 1))
    return (y, c_fin, n_fin.reshape(batch, h_a, DK_A), m_fin.reshape(batch, h_a))


def _t5_bucket(dist):
    max_exact = N_BUCKETS // 2
    d = np.maximum(dist, 1).astype(np.float64)
    large = max_exact + (np.log(d / max_exact) / math.log(MAX_DISTANCE / max_exact)
                         * (N_BUCKETS - max_exact)).astype(np.int32)
    large = np.minimum(large, N_BUCKETS - 1)
    return np.where(dist < max_exact, dist, large).astype(np.int32)


def _prompt_bias_rows(rel_bias):
    j = np.arange(2 * N_STEPS)
    valid = j <= N_STEPS
    rows = []
    for (_, r) in PATTERNS:
        bucket = _t5_bucket(np.clip(N_STEPS - j, 0, N_STEPS) * r)
        rows.append(jnp.where(valid[:, None], rel_bias[bucket].astype(F32), NEG_INF))
    return jnp.stack(rows, 0).transpose(2, 0, 1)[:, :, None, :]


def _attn_prompt_kernel(q_ref, k_ref, v_ref, bias_ref, y_ref, acc_scr, m_scr, l_scr, *, seq):
    scale = HEAD_DIM ** -0.5
    nw = N_STEPS
    n_blocks = seq // nw

    for p, (_, r) in enumerate(PATTERNS):
        nb = n_blocks // r

        def rows(g):
            c, n = divmod(g, nb)
            start = c + n * nw * r
            return pl.ds(start, nw, stride=r) if r > 1 else pl.ds(start, nw)

        tile = pltpu.roll(jnp.broadcast_to(bias_ref[0, p], (nw, 2 * nw)), 0, 1,
                          stride=1, stride_axis=0)
        tile_cur = tile[:, nw:]
        qs = [q_ref[rows(g), :].astype(BF16) for g in range(n_blocks)]
        ks = [k_ref[rows(g), :].astype(BF16) for g in range(n_blocks)]
        vs = [v_ref[rows(g), :].astype(BF16) for g in range(n_blocks)]
        for g in range(n_blocks):
            if g % nb:
                kk = jnp.concatenate([ks[g - 1], ks[g]], axis=0)
                vv = jnp.concatenate([vs[g - 1], vs[g]], axis=0)
                bias = tile
            else:
                kk, vv, bias = ks[g], vs[g], tile_cur
            s = lax.dot_general(qs[g], kk, _NT, preferred_element_type=F32)
            s = s * scale + bias
            m = jnp.max(s, axis=1, keepdims=True)
            e = jnp.exp(s - m)
            l = jnp.sum(e, axis=1, keepdims=True)
            acc = jnp.dot(e.astype(BF16), vv, preferred_element_type=F32)
            acc_scr[p, rows(g), :] = acc
            m_scr[p, rows(g), :] = jnp.broadcast_to(m, (nw, HEAD_DIM))
            l_scr[p, rows(g), :] = jnp.broadcast_to(l, (nw, HEAD_DIM))

    rows_per_step = 256

    def merge(t, carry):
        rws = pl.ds(pl.multiple_of(t * rows_per_step, rows_per_step), rows_per_step)
        m0, m1, m2 = m_scr[0, rws, :], m_scr[1, rws, :], m_scr[2, rws, :]
        mx = jnp.maximum(jnp.maximum(m0, m1), m2)
        w0, w1, w2 = jnp.exp(m0 - mx), jnp.exp(m1 - mx), jnp.exp(m2 - mx)
        num = w0 * acc_scr[0, rws, :] + w1 * acc_scr[1, rws, :] + w2 * acc_scr[2, rws, :]
        den = w0 * l_scr[0, rws, :] + w1 * l_scr[1, rws, :] + w2 * l_scr[2, rws, :]
        y_ref[rws, :] = (num / den).astype(y_ref.dtype)
        return carry
    lax.fori_loop(0, seq // rows_per_step, merge, 0)


def _attn_prompt(q, k_all, v_all, layer, bias_rows, batch, seq):
    n_pat = len(PATTERNS)
    kernel = functools.partial(_attn_prompt_kernel, seq=seq)
    return pl.pallas_call(
        kernel,
        grid=(batch, N_HEADS_B),
        in_specs=[
            pl.BlockSpec((seq, HEAD_DIM), lambda b, h: (b, h)),
            pl.BlockSpec((None, seq, HEAD_DIM), lambda b, h: (layer, b, h)),
            pl.BlockSpec((None, seq, HEAD_DIM), lambda b, h: (layer, b, h)),
            pl.BlockSpec((1, n_pat, 1, 2 * N_STEPS), lambda b, h: (h, 0, 0, 0)),
        ],
        out_specs=pl.BlockSpec((seq, HEAD_DIM), lambda b, h: (b, h)),
        out_shape=jax.ShapeDtypeStruct((batch * seq, MIX_B), BF16),
        scratch_shapes=[pltpu.VMEM((n_pat, seq, HEAD_DIM), F32),
                        pltpu.VMEM((n_pat, seq, HEAD_DIM), F32),
                        pltpu.VMEM((n_pat, seq, HEAD_DIM), F32)],
        compiler_params=_params("parallel", "parallel"),
        name="attn_prompt",
    )(q, k_all, v_all, bias_rows)


def _sample_bias(rel_bias, wbuf, t_new):
    n = wbuf + t_new
    delta = np.arange(n)
    rows = []
    for (w, r) in PATTERNS:
        valid = (delta % r == 0) & (delta // r <= w // r)
        b = rel_bias[_t5_bucket(np.minimum(delta, w))].astype(F32)
        rows.append(jnp.where(valid[:, None], b, NEG_INF))
    f = jnp.stack(rows, 0).transpose(2, 0, 1)
    f_rev = jnp.pad(f[:, :, ::-1], ((0, 0), (0, 0), (0, t_new)), constant_values=NEG_INF)
    bias_c = jnp.stack([f_rev[:, :, t_new - 1 - t: t_new - 1 - t + wbuf] for t in range(t_new)], axis=2)
    bias_n = jnp.stack([f_rev[:, :, n - 1 - t: n - 1 - t + t_new] for t in range(t_new)], axis=2)
    return bias_c, bias_n


def _attn_sample_kernel(q_ref, kn_ref, vn_ref, kc_ref, vc_ref, kx_ref, vx_ref, bc_ref, bn_ref, *refs,
                        n_chunks):
    y_ref, ko_ref, vo_ref, m_scr, l_scr, acc_scr = refs[-6:]
    scale = HEAD_DIM ** -0.5
    n_pat = len(PATTERNS)
    chunk = pl.program_id(1)
    t_new = q_ref.shape[0]
    rows = kc_ref.shape[0]
    shift = t_new * N_HEADS_B
    tokens = rows // N_HEADS_B

    ko_ref[0:rows - shift, :] = kc_ref[shift:rows, :]
    vo_ref[0:rows - shift, :] = vc_ref[shift:rows, :]

    @pl.when(chunk < n_chunks - 1)
    def _():
        ko_ref[rows - shift:rows, :] = kx_ref[...]
        vo_ref[rows - shift:rows, :] = vx_ref[...]

    @pl.when(chunk == n_chunks - 1)
    def _():
        for h in range(N_HEADS_B):
            dst = pl.ds(rows - shift + h, t_new, stride=N_HEADS_B)
            ko_ref[dst, :] = kn_ref[:, h * HEAD_DIM:(h + 1) * HEAD_DIM]
            vo_ref[dst, :] = vn_ref[:, h * HEAD_DIM:(h + 1) * HEAD_DIM]

    @pl.when(chunk == 0)
    def _():
        m_scr[...] = jnp.full(m_scr.shape, NEG_INF, F32)
        l_scr[...] = jnp.zeros(l_scr.shape, F32)
        acc_scr[...] = jnp.zeros(acc_scr.shape, F32)

    def update(h, q, k, v, biases):
        s = lax.dot_general(q, k, _NT, preferred_element_type=F32) * scale
        zs = [s + b for b in biases]
        zmax = jnp.max(zs[0], axis=1, keepdims=True)
        for z in zs[1:]:
            zmax = jnp.maximum(zmax, jnp.max(z, axis=1, keepdims=True))
        m_old = m_scr[h]
        m_new = jnp.maximum(m_old, zmax)
        alpha = jnp.exp(m_old - m_new)
        pt = jnp.exp(zs[0] - m_new)
        for z in zs[1:]:
            pt = pt + jnp.exp(z - m_new)
        m_scr[h] = m_new
        l_scr[h] = alpha * l_scr[h] + jnp.sum(pt, axis=1, keepdims=True)
        acc_scr[h] = alpha * acc_scr[h] + jnp.dot(pt.astype(BF16), v, preferred_element_type=F32)

    for h in range(N_HEADS_B):
        head_rows = pl.ds(h, tokens, stride=N_HEADS_B)
        q = q_ref[:, h * HEAD_DIM:(h + 1) * HEAD_DIM].astype(BF16)
        update(h, q, kc_ref[head_rows, :].astype(BF16), vc_ref[head_rows, :].astype(BF16),
               [bc_ref[h, p] for p in range(n_pat)])

    @pl.when(chunk == n_chunks - 1)
    def _():
        for h in range(N_HEADS_B):
            cols = slice(h * HEAD_DIM, (h + 1) * HEAD_DIM)
            q = q_ref[:, cols].astype(BF16)
            update(h, q, kn_ref[:, cols].astype(BF16), vn_ref[:, cols].astype(BF16),
                   [bn_ref[h, p] for p in range(n_pat)])
            y_ref[:, cols] = (acc_scr[h] / l_scr[h]).astype(y_ref.dtype)


def _attn_sample(q, kn, vn, cache_k, cache_v, layer, bias_c, bias_n, batch, t_new, prev_k, prev_v,
                 n_chunks=4):
    depth, _, all_rows, _ = cache_k.shape
    rows = all_rows // n_chunks
    shift = t_new * N_HEADS_B
    n_pat = len(PATTERNS)
    chunk_spec = pl.BlockSpec((None, None, rows, HEAD_DIM), lambda b, c: (layer, b, c, 0))
    next_spec = pl.BlockSpec((None, None, shift, HEAD_DIM),
                             lambda b, c: (layer, b, jnp.minimum(c + 1, n_chunks - 1) * (rows // shift), 0))
    row_spec = pl.BlockSpec((t_new, MIX_B), lambda b, c: (b, 0))
    in_specs = [row_spec, row_spec, row_spec, chunk_spec, chunk_spec, next_spec, next_spec,
                pl.BlockSpec((N_HEADS_B, n_pat, t_new, rows // N_HEADS_B), lambda b, c: (0, 0, 0, c)),
                pl.BlockSpec((N_HEADS_B, n_pat, t_new, t_new), lambda b, c: (0, 0, 0, 0))]
    args = [q, kn, vn, cache_k, cache_v, cache_k, cache_v, bias_c, bias_n]
    aliases = {}
    if prev_k is not None:
        in_specs += [pl.BlockSpec(memory_space=pl.ANY), pl.BlockSpec(memory_space=pl.ANY)]
        args += [prev_k, prev_v]
        aliases = {9: 1, 10: 2}
    cache_shape = jax.ShapeDtypeStruct(cache_k.shape, F32)
    return pl.pallas_call(
        functools.partial(_attn_sample_kernel, n_chunks=n_chunks),
        grid=(batch, n_chunks),
        in_specs=in_specs,
        out_specs=[row_spec, chunk_spec, chunk_spec],
        out_shape=[jax.ShapeDtypeStruct((batch * t_new, MIX_B), F32), cache_shape, cache_shape],
        scratch_shapes=[pltpu.VMEM((N_HEADS_B, t_new, 1), F32),
                        pltpu.VMEM((N_HEADS_B, t_new, 1), F32),
                        pltpu.VMEM((N_HEADS_B, t_new, HEAD_DIM), F32)],
        input_output_aliases=aliases,
        compiler_params=_params("parallel", "arbitrary"),
        name="attn_sample",
    )(*args)


def kernel(x_prompt, x_sample, state_mlstm_C, state_mlstm_n, state_mlstm_m, cache_win_k, cache_win_v,
           g_ffn1, w_ffn1_gate, w_ffn1_up, w_ffn1_down, g_mix, w_in, b_igate, b_fgate, g_head, w_out,
           rel_bias, g_ffn2, w_ffn2_gate, w_ffn2_up, w_ffn2_down, g_final):
    bp, sp, d = x_prompt.shape
    bs, ts, _ = x_sample.shape
    depth = g_mix.shape[0]
    wbuf = cache_win_k.shape[2]
    mp, ms = bp * sp, bs * ts
    m_all = mp + ms
    d_ff = w_ffn1_gate.shape[2]
    tm_big = _row_tile(m_all, (1376, 1024, 512, 64))
    tm_down = _row_tile(m_all, (688, 512, 64))

    x = jnp.concatenate([x_prompt.reshape(mp, d), x_sample.reshape(ms, d)], axis=0)
    zc = jnp.zeros((bp, N_HEADS_A, DK_A, DV_A), F32)
    zn = jnp.zeros((bp, N_HEADS_A, DK_A), F32)
    zm = jnp.zeros((bp, N_HEADS_A), F32)
    bias_rows = _prompt_bias_rows(rel_bias)
    bias_c, bias_n = _sample_bias(rel_bias, wbuf, ts)
    cache_k = cache_win_k.reshape(depth, bs, wbuf * N_HEADS_B, HEAD_DIM)
    cache_v = cache_win_v.reshape(depth, bs, wbuf * N_HEADS_B, HEAD_DIM)
    w_in_t = jnp.swapaxes(w_in, 1, 2)

    def ffn(x, xg, ssq, wg, wu, wd, l, next_gain):
        a, wd16 = _ffn_up(xg, ssq, wg, wu, wd, l, tm_big)
        w_spec = pl.BlockSpec((d_ff, 256), lambda i, j: (0, j))
        return _matmul_res(a, w_spec, wd16, x, 0.5, tm_down, 256, next_gain)

    p_c, p_n, p_m, s_c, s_n, s_m = [], [], [], [], [], []
    pk = pv = sk = sv = None
    xg, ssq = _gain_ssq(x, g_ffn1[0])
    for l in range(depth):
        x, xg, ssq = ffn(x, xg, ssq, w_ffn1_gate, w_ffn1_up, w_ffn1_down, l, g_mix[l])

        pa = _proj_nt(xg, ssq, w_in_t, l, 0, A_WIDTH, 256, tm_big)
        gates = _proj_nt(xg, ssq, w_in_t, l, A_WIDTH // LANES, LANES, LANES, tm_big)
        gate_bias = jnp.pad(jnp.concatenate([b_igate[l], b_fgate[l]]),
                            (0, LANES - GATE_COLS)).reshape(1, LANES)

        ya_p, c_, n_, m_ = _mlstm(pa, gates, gate_bias, g_head, l, zc, zn, zm,
                                  0, bp, sp, PROMPT_CHUNK, BF16)
        p_c.append(c_); p_n.append(n_); p_m.append(m_)
        ya_s, c_, n_, m_ = _mlstm(pa, gates, gate_bias, g_head, l, state_mlstm_C[l], state_mlstm_n[l],
                                  state_mlstm_m[l], mp, bs, ts, ts, F32)
        s_c.append(c_); s_n.append(n_); s_m.append(m_)

        q_p, pk, pv = _qkv_prompt(xg, ssq, w_in_t, l, depth, mp, pk, pv)
        yb_p = _attn_prompt(q_p, pk, pv, l, bias_rows, bp, sp)
        q_s, k_s, v_s = _qkv_sample(xg, ssq, w_in_t, l, mp, ms)
        yb_s, sk, sv = _attn_sample(q_s, k_s, v_s, cache_k, cache_v, l, bias_c, bias_n, bs, ts, sk, sv)

        y = jnp.concatenate([jnp.concatenate([ya_p, yb_p], axis=1),
                             jnp.concatenate([ya_s, yb_s], axis=1).astype(BF16)], axis=0)
        wo_spec = pl.BlockSpec((None, d, 256), lambda i, j, l=l: (l, 0, j))
        x, xg, ssq = _matmul_res(y, wo_spec, w_out, x, 1.0, tm_big, 256, g_ffn2[l])

        if l + 1 < depth:
            x, xg, ssq = ffn(x, xg, ssq, w_ffn2_gate, w_ffn2_up, w_ffn2_down, l, g_ffn1[l + 1])
        else:
            x = ffn(x, xg, ssq, w_ffn2_gate, w_ffn2_up, w_ffn2_down, l, None)

    yp = _rmsnorm(x, g_final, F32, 0, mp).reshape(bp, sp, d)
    ys = _rmsnorm(x, g_final, F32, mp, ms).reshape(bs, ts, d)
    return (yp, ys, jnp.stack(p_c), jnp.stack(p_n), jnp.stack(p_m),
            pk.reshape(depth, bp, sp, N_HEADS_B, HEAD_DIM), pv.reshape(depth, bp, sp, N_HEADS_B, HEAD_DIM),
            jnp.stack(s_c), jnp.stack(s_n), jnp.stack(s_m),
            sk.reshape(depth, bs, wbuf, N_HEADS_B, HEAD_DIM), sv.reshape(depth, bs, wbuf, N_HEADS_B, HEAD_DIM))
```

```python
import functools
import math

import numpy as np
import jax
import jax.numpy as jnp
from jax import lax
from jax.experimental import pallas as pl
from jax.experimental.pallas import tpu as pltpu

F32 = jnp.float32
BF16 = jnp.bfloat16

D_MODEL = 4096
N_HEADS_A = 8
DK_A = 128
DV_A = 256
MIX_A = N_HEADS_A * DV_A
N_HEADS_B = 16
HEAD_DIM = 128
MIX_B = N_HEADS_B * HEAD_DIM
PATTERNS = ((128, 1), (512, 4), (2048, 16))
N_STEPS = 128
N_BUCKETS = 32
MAX_DISTANCE = 2048
EPS = 1e-6
NEG_INF = -1e30
A_WIDTH = 2 * N_HEADS_A * DK_A + 2 * MIX_A
GATE_COLS = 2 * N_HEADS_A
LANES = 128
SUBLANES = 8
PROMPT_CHUNK = 512

VMEM_LIMIT = 56 * 1024 * 1024


def _params(*sem):
    return pltpu.CompilerParams(dimension_semantics=sem, vmem_limit_bytes=VMEM_LIMIT)


def _row_tile(m, candidates):
    for t in candidates:
        if m % t == 0:
            return t
    raise ValueError(f"no row tile for {m} rows among {candidates}")


def _inv_rms(ssq_ref, d):
    return lax.rsqrt(ssq_ref[:, 0:1] * (1.0 / d) + EPS)


def _rmsnorm_kernel(x_ref, g_ref, o_ref):
    x = x_ref[...]
    ms = jnp.mean(x * x, axis=-1, keepdims=True)
    o_ref[...] = (x * lax.rsqrt(ms + EPS) * g_ref[...]).astype(o_ref.dtype)


def _rmsnorm(x, g, out_dtype, row_start, n_rows):
    _, d = x.shape
    tm = _row_tile(math.gcd(n_rows, row_start) if row_start else n_rows, (256, 192, 128, 64))
    off = row_start // tm
    return pl.pallas_call(
        _rmsnorm_kernel,
        grid=(n_rows // tm,),
        in_specs=[pl.BlockSpec((tm, d), lambda i: (off + i, 0)),
                  pl.BlockSpec((1, d), lambda i: (0, 0))],
        out_specs=pl.BlockSpec((tm, d), lambda i: (i, 0)),
        out_shape=jax.ShapeDtypeStruct((n_rows, d), out_dtype),
        compiler_params=_params("parallel"),
        name="rmsnorm",
    )(x, g.reshape(1, d))


def _gain_ssq_kernel(x_ref, g_ref, xg_ref, ssq_ref):
    x = x_ref[...]
    xg_ref[...] = (x * g_ref[...]).astype(xg_ref.dtype)
    ssq_ref[...] = jnp.broadcast_to(jnp.sum(x * x, axis=-1, keepdims=True), ssq_ref.shape)


def _gain_ssq(x, g):
    m, d = x.shape
    tm = _row_tile(m, (256, 192, 128, 64))
    return pl.pallas_call(
        _gain_ssq_kernel,
        grid=(m // tm,),
        in_specs=[pl.BlockSpec((tm, d), lambda i: (i, 0)),
                  pl.BlockSpec((1, d), lambda i: (0, 0))],
        out_specs=[pl.BlockSpec((tm, d), lambda i: (i, 0)),
                   pl.BlockSpec((tm, LANES), lambda i: (i, 0))],
        out_shape=[jax.ShapeDtypeStruct((m, d), BF16), jax.ShapeDtypeStruct((m, LANES), F32)],
        compiler_params=_params("parallel"),
        name="gain_ssq",
    )(x, g.reshape(1, d))


_NT = (((1,), (1,)), ((), ()))


def _row_chunks(ref, n):
    rows = ref.shape[0] // n
    return [slice(c * rows, (c + 1) * rows) for c in range(n)]


def _proj_nt_kernel(x_ref, ssq_ref, w_ref, o_ref, *, row_chunks):
    w = w_ref[...].astype(BF16)
    for rs in _row_chunks(x_ref, row_chunks):
        acc = lax.dot_general(x_ref[rs, :], w, _NT, preferred_element_type=F32)
        o_ref[rs, :] = acc * lax.rsqrt(ssq_ref[rs, 0:1] * (1.0 / x_ref.shape[1]) + EPS)


def _proj_nt(xg, ssq, wt, layer, row_block0, n_cols, tn, tm):
    m, k = xg.shape
    return pl.pallas_call(
        functools.partial(_proj_nt_kernel, row_chunks=2 if tm % 32 == 0 else 1),
        grid=(m // tm, n_cols // tn),
        in_specs=[pl.BlockSpec((tm, k), lambda i, j: (i, 0)),
                  pl.BlockSpec((tm, LANES), lambda i, j: (i, 0)),
                  pl.BlockSpec((None, tn, k), lambda i, j: (layer, row_block0 + j, 0))],
        out_specs=pl.BlockSpec((tm, tn), lambda i, j: (i, j)),
        out_shape=jax.ShapeDtypeStruct((m, n_cols), F32),
        compiler_params=_params("parallel", "arbitrary"),
        name="proj_nt",
    )(xg, ssq, wt)


def _qkv_kernel(x_ref, ssq_ref, wq_ref, wk_ref, wv_ref, *refs):
    q_ref, k_ref, v_ref = refs[-3:]
    x = x_ref[...]
    inv = _inv_rms(ssq_ref, x_ref.shape[1])
    for w_ref, o_ref in ((wq_ref, q_ref), (wk_ref, k_ref), (wv_ref, v_ref)):
        acc = lax.dot_general(x, w_ref[...].astype(BF16), _NT, preferred_element_type=F32)
        o_ref[...] = acc * inv


def _qkv_weight_specs(wt2d_rows, layer, tn, k_dim):
    base = layer * wt2d_rows + A_WIDTH + GATE_COLS
    specs = []
    for part in range(3):
        def index(*ids, part=part):
            j = ids[-1]
            return (pl.multiple_of(base + part * MIX_B + j * tn, SUBLANES), 0)
        specs.append(pl.BlockSpec((pl.Element(tn), pl.Element(k_dim)), index))
    return specs


def _qkv_prompt(xg, ssq, wt, layer, depth, n_rows, prev_k, prev_v, tm=1024, tn=256):
    k_dim = xg.shape[1]
    nb = MIX_B // tn
    n_all = wt.shape[1]
    kv_shape = jax.ShapeDtypeStruct((depth, n_rows, MIX_B), F32)
    in_specs = ([pl.BlockSpec((tm, k_dim), lambda i, j: (i, 0)),
                 pl.BlockSpec((tm, LANES), lambda i, j: (i, 0))]
                + _qkv_weight_specs(n_all, layer, tn, k_dim))
    wt2d = wt.reshape(depth * n_all, k_dim)
    args = [xg, ssq, wt2d, wt2d, wt2d]
    aliases = {}
    if prev_k is not None:
        in_specs += [pl.BlockSpec(memory_space=pl.ANY), pl.BlockSpec(memory_space=pl.ANY)]
        args += [prev_k, prev_v]
        aliases = {5: 1, 6: 2}
    return pl.pallas_call(
        _qkv_kernel,
        grid=(n_rows // tm, nb),
        in_specs=in_specs,
        out_specs=[pl.BlockSpec((tm, tn), lambda i, j: (i, j)),
                   pl.BlockSpec((None, tm, tn), lambda i, j: (layer, i, j)),
                   pl.BlockSpec((None, tm, tn), lambda i, j: (layer, i, j))],
        out_shape=[jax.ShapeDtypeStruct((n_rows, MIX_B), F32), kv_shape, kv_shape],
        input_output_aliases=aliases,
        compiler_params=_params("parallel", "arbitrary"),
        name="qkv_prompt",
    )(*args)


def _qkv_sample(xg, ssq, wt, layer, row_start, n_rows, tn=256):
    k_dim = xg.shape[1]
    nb = MIX_B // tn
    depth, n_all, _ = wt.shape
    off = row_start // n_rows
    out = jax.ShapeDtypeStruct((n_rows, MIX_B), F32)
    wt2d = wt.reshape(depth * n_all, k_dim)
    return pl.pallas_call(
        _qkv_kernel,
        grid=(nb,),
        in_specs=([pl.BlockSpec((n_rows, k_dim), lambda j: (off, 0)),
                   pl.BlockSpec((n_rows, LANES), lambda j: (off, 0))]
                  + _qkv_weight_specs(n_all, layer, tn, k_dim)),
        out_specs=[pl.BlockSpec((n_rows, tn), lambda j: (0, j))] * 3,
        out_shape=[out, out, out],
        compiler_params=_params("arbitrary"),
        name="qkv_sample",
    )(xg, ssq, wt2d, wt2d, wt2d)


def _ffn_up_kernel(x_ref, ssq_ref, wg_ref, wu_ref, wd_ref, o_ref, wd16_ref, *, row_chunks):
    wg = wg_ref[...].astype(BF16)
    wu = wu_ref[...].astype(BF16)
    for rs in _row_chunks(x_ref, row_chunks):
        x = x_ref[rs, :]
        inv = lax.rsqrt(ssq_ref[rs, 0:1] * (1.0 / x_ref.shape[1]) + EPS)
        g = jnp.dot(x, wg, preferred_element_type=F32) * inv
        u = jnp.dot(x, wu, preferred_element_type=F32) * inv
        o_ref[rs, :] = (g * jax.nn.sigmoid(g) * u).astype(o_ref.dtype)
    wd16_ref[...] = wd_ref[...].astype(wd16_ref.dtype)


def _ffn_up(xg, ssq, wg, wu, wd, layer, tm, tn=256, wd_rows=64, row_chunks=2):
    m, k = xg.shape
    n = wg.shape[2]
    n_j = n // tn
    n_slabs = wd.shape[1] // wd_rows
    assert wd.shape[1] % wd_rows == 0 and n_slabs <= (m // tm) * n_j

    def slab(i, j):
        return jnp.minimum(i * n_j + j, n_slabs - 1)

    return pl.pallas_call(
        functools.partial(_ffn_up_kernel, row_chunks=row_chunks),
        grid=(m // tm, n_j),
        in_specs=[pl.BlockSpec((tm, k), lambda i, j: (i, 0)),
                  pl.BlockSpec((tm, LANES), lambda i, j: (i, 0)),
                  pl.BlockSpec((None, k, tn), lambda i, j: (layer, 0, j)),
                  pl.BlockSpec((None, k, tn), lambda i, j: (layer, 0, j)),
                  pl.BlockSpec((None, wd_rows, wd.shape[2]), lambda i, j: (layer, slab(i, j), 0))],
        out_specs=[pl.BlockSpec((tm, tn), lambda i, j: (i, j)),
                   pl.BlockSpec((wd_rows, wd.shape[2]), lambda i, j: (slab(i, j), 0))],
        out_shape=[jax.ShapeDtypeStruct((m, n), BF16),
                   jax.ShapeDtypeStruct(wd.shape[1:], BF16)],
        compiler_params=_params("arbitrary", "arbitrary"),
        name="ffn_up",
    )(xg, ssq, wg, wu, wd)


def _matmul_res_kernel(*refs, scale, emit_norm, n_lhs):
    a_refs, w_refs, (r_ref, *refs) = refs[:n_lhs], refs[n_lhs:2 * n_lhs], refs[2 * n_lhs:]
    acc = None
    for a_ref, w_ref in zip(a_refs, w_refs):
        prod = jnp.dot(a_ref[...], w_ref[...].astype(BF16), preferred_element_type=F32)
        acc = prod if acc is None else acc + prod
    x_new = r_ref[...] + scale * acc
    if not emit_norm:
        (o_ref,) = refs
        o_ref[...] = x_new
        return
    g_ref, o_ref, xg_ref, ssq_ref = refs
    o_ref[...] = x_new
    xg_ref[...] = (x_new * g_ref[...]).astype(xg_ref.dtype)
    part = jnp.broadcast_to(jnp.sum(x_new * x_new, axis=-1, keepdims=True), ssq_ref.shape)

    @pl.when(pl.program_id(1) == 0)
    def _():
        ssq_ref[...] = part

    @pl.when(pl.program_id(1) > 0)
    def _():
        ssq_ref[...] += part


def _matmul_res(lhs, w_specs, ws, res, scale, tm, tn, next_gain=None):
    m = lhs[0].shape[0]
    n = res.shape[1]
    emit_norm = next_gain is not None
    in_specs = ([pl.BlockSpec((tm, a.shape[1]), lambda i, j: (i, 0)) for a in lhs] + list(w_specs)
                + [pl.BlockSpec((tm, tn), lambda i, j: (i, j))])
    args = list(lhs) + list(ws) + [res]
    out_specs = [pl.BlockSpec((tm, tn), lambda i, j: (i, j))]
    out_shape = [jax.ShapeDtypeStruct((m, n), F32)]
    if emit_norm:
        in_specs.append(pl.BlockSpec((1, tn), lambda i, j: (0, j)))
        args.append(next_gain.reshape(1, n))
        out_specs += [pl.BlockSpec((tm, tn), lambda i, j: (i, j)),
                      pl.BlockSpec((tm, LANES), lambda i, j: (i, 0))]
        out_shape += [jax.ShapeDtypeStruct((m, n), BF16), jax.ShapeDtypeStruct((m, LANES), F32)]
    out = pl.pallas_call(
        functools.partial(_matmul_res_kernel, scale=scale, emit_norm=emit_norm, n_lhs=len(lhs)),
        grid=(m // tm, n // tn),
        in_specs=in_specs,
        out_specs=out_specs,
        out_shape=out_shape,
        compiler_params=_params("parallel", "arbitrary"),
        name="matmul_res",
    )(*args)
    return out if emit_norm else out[0]


def _place_rows_kernel(a_ref, b_ref, ya_in, yb_in, ya_ref, yb_ref):
    del ya_in, yb_in
    ya_ref[...] = a_ref[...].astype(ya_ref.dtype)
    yb_ref[...] = b_ref[...].astype(yb_ref.dtype)


def _place_rows(a, b, ya, yb, row_start):
    rows = a.shape[0]
    blk = row_start // rows
    return pl.pallas_call(
        _place_rows_kernel,
        grid=(1,),
        in_specs=[pl.BlockSpec(a.shape, lambda i: (0, 0)), pl.BlockSpec(b.shape, lambda i: (0, 0)),
                  pl.BlockSpec(memory_space=pl.ANY), pl.BlockSpec(memory_space=pl.ANY)],
        out_specs=[pl.BlockSpec(a.shape, lambda i: (blk, 0)), pl.BlockSpec(b.shape, lambda i: (blk, 0))],
        out_shape=[jax.ShapeDtypeStruct(ya.shape, ya.dtype), jax.ShapeDtypeStruct(yb.shape, yb.dtype)],
        input_output_aliases={2: 0, 3: 1},
        compiler_params=_params("arbitrary"),
        name="place_rows",
    )(a, b, ya, yb)


def _mlstm_kernel(q_ref, k_ref, v_ref, o_ref, g_ref, gb_ref, gh_ref, c0_ref, n0_ref, m0_ref,
                  y_ref, c_out_ref, n_out_ref, m_out_ref, c_scr, n_scr, m_scr, *, chunk, n_chunks):
    c_idx = pl.program_id(1)
    L = chunk

    @pl.when(c_idx == 0)
    def _():
        c_scr[...] = c0_ref[0]
        n_scr[...] = n0_ref[0]
        m_scr[...] = m0_ref[0]

    gates = g_ref[...] + gb_ref[...]
    lane = lax.broadcasted_iota(jnp.int32, gates.shape, 1)
    row = lax.broadcasted_iota(jnp.int32, (L, L), 0)
    col = lax.broadcasted_iota(jnp.int32, (L, L), 1)
    eye = row == col
    causal = col <= row
    anti = row <= col

    for head in range(N_HEADS_A):
        qs = slice(head * DK_A, (head + 1) * DK_A)
        vs = slice(head * DV_A, (head + 1) * DV_A)
        i_col = jnp.sum(jnp.where(lane == head, gates, 0.0), axis=1, keepdims=True)
        f_col = jnp.sum(jnp.where(lane == head + N_HEADS_A, gates, 0.0), axis=1, keepdims=True)
        lf_col = jax.nn.log_sigmoid(f_col)
        lf_row = jnp.sum(jnp.where(eye, lf_col, 0.0), axis=0, keepdims=True)
        i_row = jnp.sum(jnp.where(eye, i_col, 0.0), axis=0, keepdims=True)
        b_col = jnp.sum(jnp.where(causal, lf_row, 0.0), axis=1, keepdims=True)
        b_row = jnp.sum(jnp.where(anti, lf_col, 0.0), axis=0, keepdims=True)

        m_prev = m_scr[head]
        dmat = jnp.where(causal, b_col - b_row + i_row, -jnp.inf)
        inter = b_col + m_prev
        mt = jnp.maximum(inter, jnp.max(dmat, axis=1, keepdims=True))

        q32 = q_ref[:, qs]
        k32 = k_ref[:, qs] * (DK_A ** -0.5)
        q = q32.astype(BF16)
        k = k32.astype(BF16)
        v = v_ref[:, vs].astype(BF16)
        c_prev = c_scr[head]
        n_prev = n_scr[head]

        s = lax.dot_general(q, k, _NT, preferred_element_type=F32)
        w = s * jnp.exp(dmat - mt)
        a = jnp.exp(inter - mt)
        num = (jnp.dot(w.astype(BF16), v, preferred_element_type=F32)
               + a * jnp.dot(q, c_prev.astype(BF16), preferred_element_type=F32))
        den = (jnp.sum(w, axis=1, keepdims=True)
               + a * jnp.sum(q32 * n_prev, axis=1, keepdims=True))
        h = num / jnp.maximum(jnp.abs(den), jnp.exp(-mt))

        m_new = mt[L - 1:L, :]
        b_last = b_col[L - 1:L, :]
        g = jnp.exp(b_last - b_col + i_col - m_new)
        dec = jnp.exp(b_last + m_prev - m_new)
        gk = g * k32
        c_new = dec * c_prev + lax.dot_general(gk.astype(BF16), v, (((0,), (0,)), ((), ())),
                                               preferred_element_type=F32)
        n_new = dec * n_prev + jnp.sum(gk, axis=0, keepdims=True)
        c_scr[head] = c_new
        n_scr[head] = n_new
        m_scr[head] = m_new

        hn = h * lax.rsqrt(jnp.mean(h * h, axis=-1, keepdims=True) + EPS)
        y_ref[:, vs] = (hn * gh_ref[:, vs] * jax.nn.sigmoid(o_ref[:, vs])).astype(y_ref.dtype)

    @pl.when(c_idx == n_chunks - 1)
    def _():
        c_out_ref[0] = c_scr[...]
        n_out_ref[0] = n_scr[...]
        m_out_ref[0] = m_scr[...]


def _mlstm(pa, gates, gate_bias, g_head, layer, c0, n0, m0, row_start, batch, seq, chunk, out_dtype,
           out_rows=None):
    n_chunks = seq // chunk
    m_rows = batch * seq
    qk_w = N_HEADS_A * DK_A
    r0 = row_start // chunk
    h_a = N_HEADS_A

    def rows(b, c):
        return r0 + b * n_chunks + c

    def state_specs():
        return [pl.BlockSpec((1, h_a, DK_A, DV_A), lambda b, c: (b, 0, 0, 0)),
                pl.BlockSpec((1, h_a, 1, DK_A), lambda b, c: (b, 0, 0, 0)),
                pl.BlockSpec((1, h_a, 1, 1), lambda b, c: (b, 0, 0, 0))]

    kernel = functools.partial(_mlstm_kernel, chunk=chunk, n_chunks=n_chunks)
    y, c_fin, n_fin, m_fin = pl.pallas_call(
        kernel,
        grid=(batch, n_chunks),
        in_specs=[
            pl.BlockSpec((chunk, qk_w), lambda b, c: (rows(b, c), 0)),
            pl.BlockSpec((chunk, qk_w), lambda b, c: (rows(b, c), 1)),
            pl.BlockSpec((chunk, MIX_A), lambda b, c: (rows(b, c), 2 * qk_w // MIX_A)),
            pl.BlockSpec((chunk, MIX_A), lambda b, c: (rows(b, c), 2 * qk_w // MIX_A + 1)),
            pl.BlockSpec((chunk, LANES), lambda b, c: (rows(b, c), 0)),
            pl.BlockSpec((1, LANES), lambda b, c: (0, 0)),
            pl.BlockSpec((None, 1, MIX_A), lambda b, c: (layer, 0, 0)),
        ] + state_specs(),
        out_specs=[pl.BlockSpec((chunk, MIX_A), lambda b, c: (b * n_chunks + c, 0))] + state_specs(),
        out_shape=[
            jax.ShapeDtypeStruct((out_rows or m_rows, MIX_A), out_dtype),
            jax.ShapeDtypeStruct((batch, h_a, DK_A, DV_A), F32),
            jax.ShapeDtypeStruct((batch, h_a, 1, DK_A), F32),
            jax.ShapeDtypeStruct((batch, h_a, 1, 1), F32),
        ],
        scratch_shapes=[pltpu.VMEM((h_a, DK_A, DV_A), F32),
                        pltpu.VMEM((h_a, 1, DK_A), F32),
                        pltpu.VMEM((h_a, 1, 1), F32)],
        compiler_params=_params("parallel", "arbitrary"),
        name="mlstm",
    )(pa, pa, pa, pa, gates, gate_bias, g_head.reshape(g_head.shape[0], 1, MIX_A),
      c0, n0.reshape(batch, h_a, 1, DK_A), m0.reshape(batch, h_a, 1, 1))
    return (y, c_fin, n_fin.reshape(batch, h_a, DK_A), m_fin.reshape(batch, h_a))


def _t5_bucket(dist):
    max_exact = N_BUCKETS // 2
    d = np.maximum(dist, 1).astype(np.float64)
    large = max_exact + (np.log(d / max_exact) / math.log(MAX_DISTANCE / max_exact)
                         * (N_BUCKETS - max_exact)).astype(np.int32)
    large = np.minimum(large, N_BUCKETS - 1)
    return np.where(dist < max_exact, dist, large).astype(np.int32)


def _prompt_bias_rows(rel_bias):
    j = np.arange(2 * N_STEPS)
    valid = j <= N_STEPS
    rows = []
    for (_, r) in PATTERNS:
        bucket = _t5_bucket(np.clip(N_STEPS - j, 0, N_STEPS) * r)
        rows.append(jnp.where(valid[:, None], rel_bias[bucket].astype(F32), NEG_INF))
    return jnp.stack(rows, 0).transpose(2, 0, 1)[:, :, None, :]


def _attn_prompt_kernel(q_ref, k_ref, v_ref, bias_ref, y_ref, acc_scr, m_scr, l_scr, *, seq):
    scale = HEAD_DIM ** -0.5
    nw = N_STEPS
    n_blocks = seq // nw

    for p, (_, r) in enumerate(PATTERNS):
        nb = n_blocks // r

        def rows(g):
            c, n = divmod(g, nb)
            start = c + n * nw * r
            return pl.ds(start, nw, stride=r) if r > 1 else pl.ds(start, nw)

        tile = pltpu.roll(jnp.broadcast_to(bias_ref[0, p], (nw, 2 * nw)), 0, 1,
                          stride=1, stride_axis=0)
        tile_cur = tile[:, nw:]
        qs = [q_ref[rows(g), :].astype(BF16) for g in range(n_blocks)]
        ks = [k_ref[rows(g), :].astype(BF16) for g in range(n_blocks)]
        vs = [v_ref[rows(g), :].astype(BF16) for g in range(n_blocks)]
        for g in range(n_blocks):
            if g % nb:
                kk = jnp.concatenate([ks[g - 1], ks[g]], axis=0)
                vv = jnp.concatenate([vs[g - 1], vs[g]], axis=0)
                bias = tile
            else:
                kk, vv, bias = ks[g], vs[g], tile_cur
            s = lax.dot_general(qs[g], kk, _NT, preferred_element_type=F32)
            s = s * scale + bias
            m = jnp.max(s, axis=1, keepdims=True)
            e = jnp.exp(s - m)
            l = jnp.sum(e, axis=1, keepdims=True)
            acc = jnp.dot(e.astype(BF16), vv, preferred_element_type=F32)
            acc_scr[p, rows(g), :] = acc
            m_scr[p, rows(g), :] = jnp.broadcast_to(m, (nw, HEAD_DIM))
            l_scr[p, rows(g), :] = jnp.broadcast_to(l, (nw, HEAD_DIM))

    rows_per_step = 256

    def merge(t, carry):
        rws = pl.ds(pl.multiple_of(t * rows_per_step, rows_per_step), rows_per_step)
        m0, m1, m2 = m_scr[0, rws, :], m_scr[1, rws, :], m_scr[2, rws, :]
        mx = jnp.maximum(jnp.maximum(m0, m1), m2)
        w0, w1, w2 = jnp.exp(m0 - mx), jnp.exp(m1 - mx), jnp.exp(m2 - mx)
        num = w0 * acc_scr[0, rws, :] + w1 * acc_scr[1, rws, :] + w2 * acc_scr[2, rws, :]
        den = w0 * l_scr[0, rws, :] + w1 * l_scr[1, rws, :] + w2 * l_scr[2, rws, :]
        y_ref[rws, :] = (num / den).astype(y_ref.dtype)
        return carry
    lax.fori_loop(0, seq // rows_per_step, merge, 0)


def _attn_prompt(q, k_all, v_all, layer, bias_rows, batch, seq, out_rows=None):
    n_pat = len(PATTERNS)
    kernel = functools.partial(_attn_prompt_kernel, seq=seq)
    return pl.pallas_call(
        kernel,
        grid=(batch, N_HEADS_B),
        in_specs=[
            pl.BlockSpec((seq, HEAD_DIM), lambda b, h: (b, h)),
            pl.BlockSpec((None, seq, HEAD_DIM), lambda b, h: (layer, b, h)),
            pl.BlockSpec((None, seq, HEAD_DIM), lambda b, h: (layer, b, h)),
            pl.BlockSpec((1, n_pat, 1, 2 * N_STEPS), lambda b, h: (h, 0, 0, 0)),
        ],
        out_specs=pl.BlockSpec((seq, HEAD_DIM), lambda b, h: (b, h)),
        out_shape=jax.ShapeDtypeStruct((out_rows or batch * seq, MIX_B), BF16),
        scratch_shapes=[pltpu.VMEM((n_pat, seq, HEAD_DIM), F32),
                        pltpu.VMEM((n_pat, seq, HEAD_DIM), F32),
                        pltpu.VMEM((n_pat, seq, HEAD_DIM), F32)],
        compiler_params=_params("parallel", "parallel"),
        name="attn_prompt",
    )(q, k_all, v_all, bias_rows)


def _sample_bias(rel_bias, wbuf, t_new):
    n = wbuf + t_new
    delta = np.arange(n)
    rows, valid_by_dist = [], []
    for (w, r) in PATTERNS:
        valid = (delta % r == 0) & (delta // r <= w // r)
        valid_by_dist.append(valid)
        b = rel_bias[_t5_bucket(np.minimum(delta, w))].astype(F32)
        rows.append(jnp.where(valid[:, None], b, NEG_INF))
    f = jnp.stack(rows, 0).transpose(2, 0, 1)
    f_rev = jnp.pad(f[:, :, ::-1], ((0, 0), (0, 0), (0, t_new)), constant_values=NEG_INF)
    bias_c = jnp.stack([f_rev[:, :, t_new - 1 - t: t_new - 1 - t + wbuf] for t in range(t_new)], axis=2)
    bias_n = jnp.stack([f_rev[:, :, n - 1 - t: n - 1 - t + t_new] for t in range(t_new)], axis=2)
    dist = np.arange(t_new)[:, None] + wbuf - np.arange(wbuf)[None, :]
    valid_c = np.stack([v[dist] for v in valid_by_dist], 0)
    return bias_c, bias_n, valid_c


def _far_chunks(valid_c, n_chunks, t_new):
    r_max = max(r for _, r in PATTERNS)
    wbuf = valid_c.shape[2]
    tokens = wbuf // n_chunks
    if r_max % t_new or tokens % r_max:
        return ()
    far = []
    for c in range(n_chunks):
        in_chunk = valid_c[:, :, c * tokens:(c + 1) * tokens]
        only_widest = all(not in_chunk[p].any() for p, (_, r) in enumerate(PATTERNS) if r != r_max)
        tok = np.arange(c * tokens, (c + 1) * tokens)
        inside = not in_chunk[:, :, (tok % r_max) >= t_new].any()
        if only_widest and inside:
            far.append(c)
    return tuple(far)


def _attn_sample_kernel(q_ref, kn_ref, vn_ref, kc_ref, vc_ref, kx_ref, vx_ref, bc_ref, bf_ref, bn_ref,
                        *refs, n_chunks, far_chunks, group_step):
    y_ref, ko_ref, vo_ref, m_scr, l_scr, acc_scr = refs[-6:]
    scale = HEAD_DIM ** -0.5
    n_pat = len(PATTERNS)
    chunk = pl.program_id(1)
    t_new = q_ref.shape[0]
    groups = kc_ref.shape[0]

    ko_ref[0:groups - 1] = kc_ref[1:groups]
    vo_ref[0:groups - 1] = vc_ref[1:groups]

    @pl.when(chunk < n_chunks - 1)
    def _():
        ko_ref[groups - 1] = kx_ref[0]
        vo_ref[groups - 1] = vx_ref[0]

    @pl.when(chunk == n_chunks - 1)
    def _():
        for h in range(N_HEADS_B):
            dst = pl.ds(h, t_new, stride=N_HEADS_B)
            ko_ref[groups - 1, dst, :] = kn_ref[:, h * HEAD_DIM:(h + 1) * HEAD_DIM]
            vo_ref[groups - 1, dst, :] = vn_ref[:, h * HEAD_DIM:(h + 1) * HEAD_DIM]

    @pl.when(chunk == 0)
    def _():
        m_scr[...] = jnp.full(m_scr.shape, NEG_INF, F32)
        l_scr[...] = jnp.zeros(l_scr.shape, F32)
        acc_scr[...] = jnp.zeros(acc_scr.shape, F32)

    def update(h, q, k, v, biases):
        s = lax.dot_general(q, k, _NT, preferred_element_type=F32) * scale
        zs = [s + b for b in biases]
        zmax = jnp.max(zs[0], axis=1, keepdims=True)
        for z in zs[1:]:
            zmax = jnp.maximum(zmax, jnp.max(z, axis=1, keepdims=True))
        m_old = m_scr[h]
        m_new = jnp.maximum(m_old, zmax)
        alpha = jnp.exp(m_old - m_new)
        pt = jnp.exp(zs[0] - m_new)
        for z in zs[1:]:
            pt = pt + jnp.exp(z - m_new)
        m_scr[h] = m_new
        l_scr[h] = alpha * l_scr[h] + jnp.sum(pt, axis=1, keepdims=True)
        acc_scr[h] = alpha * acc_scr[h] + jnp.dot(pt.astype(BF16), v, preferred_element_type=F32)

    def head_rows(ref, h, sparse):
        in_group = pl.ds(h, t_new, stride=N_HEADS_B)
        if sparse:
            x = ref[pl.ds(0, groups // group_step, stride=group_step), in_group, :]
        else:
            x = ref[:, in_group, :]
        return x.reshape(-1, HEAD_DIM).astype(BF16)

    def cached(sparse):
        for h in range(N_HEADS_B):
            q = q_ref[:, h * HEAD_DIM:(h + 1) * HEAD_DIM].astype(BF16)
            biases = [bf_ref[h]] if sparse else [bc_ref[h, p] for p in range(n_pat)]
            update(h, q, head_rows(kc_ref, h, sparse), head_rows(vc_ref, h, sparse), biases)

    if far_chunks:
        is_far = functools.reduce(jnp.logical_or, [chunk == c for c in far_chunks])
        pl.when(is_far)(lambda: cached(True))
        pl.when(jnp.logical_not(is_far))(lambda: cached(False))
    else:
        cached(False)

    @pl.when(chunk == n_chunks - 1)
    def _():
        for h in range(N_HEADS_B):
            cols = slice(h * HEAD_DIM, (h + 1) * HEAD_DIM)
            q = q_ref[:, cols].astype(BF16)
            update(h, q, kn_ref[:, cols].astype(BF16), vn_ref[:, cols].astype(BF16),
                   [bn_ref[h, p] for p in range(n_pat)])
            y_ref[:, cols] = (acc_scr[h] / l_scr[h]).astype(y_ref.dtype)


def _attn_sample(q, kn, vn, cache_k, cache_v, layer, bias_c, bias_n, valid_c, batch, t_new,
                 prev_k, prev_v, n_chunks=4):
    depth, _, all_groups, group_rows, _ = cache_k.shape
    groups = all_groups // n_chunks
    tokens = groups * t_new
    n_pat = len(PATTERNS)
    r_max = max(r for _, r in PATTERNS)
    far_pattern = [r for _, r in PATTERNS].index(r_max)
    far_chunks = _far_chunks(valid_c, n_chunks, t_new)
    group_step = r_max // t_new if far_chunks else 1
    n_heads, _, _, wbuf = bias_c.shape
    bias_far = bias_c[:, far_pattern].reshape(n_heads, t_new, wbuf // r_max, r_max)[..., :t_new]
    bias_far = bias_far.reshape(n_heads, t_new, wbuf // r_max * t_new)
    far_tokens = bias_far.shape[2] // n_chunks

    chunk_spec = pl.BlockSpec((None, None, groups, group_rows, HEAD_DIM), lambda b, c: (layer, b, c, 0, 0))
    next_spec = pl.BlockSpec((None, None, 1, group_rows, HEAD_DIM),
                             lambda b, c: (layer, b, jnp.minimum(c + 1, n_chunks - 1) * groups, 0, 0))
    row_spec = pl.BlockSpec((t_new, MIX_B), lambda b, c: (b, 0))
    in_specs = [row_spec, row_spec, row_spec, chunk_spec, chunk_spec, next_spec, next_spec,
                pl.BlockSpec((N_HEADS_B, n_pat, t_new, tokens), lambda b, c: (0, 0, 0, c)),
                pl.BlockSpec((N_HEADS_B, t_new, far_tokens), lambda b, c: (0, 0, c)),
                pl.BlockSpec((N_HEADS_B, n_pat, t_new, t_new), lambda b, c: (0, 0, 0, 0))]
    args = [q, kn, vn, cache_k, cache_v, cache_k, cache_v, bias_c, bias_far, bias_n]
    aliases = {}
    if prev_k is not None:
        in_specs += [pl.BlockSpec(memory_space=pl.ANY), pl.BlockSpec(memory_space=pl.ANY)]
        args += [prev_k, prev_v]
        aliases = {10: 1, 11: 2}
    cache_shape = jax.ShapeDtypeStruct(cache_k.shape, F32)
    kernel = functools.partial(_attn_sample_kernel, n_chunks=n_chunks, far_chunks=far_chunks,
                               group_step=group_step)
    return pl.pallas_call(
        kernel,
        grid=(batch, n_chunks),
        in_specs=in_specs,
        out_specs=[row_spec, chunk_spec, chunk_spec],
        out_shape=[jax.ShapeDtypeStruct((batch * t_new, MIX_B), F32), cache_shape, cache_shape],
        scratch_shapes=[pltpu.VMEM((N_HEADS_B, t_new, 1), F32),
                        pltpu.VMEM((N_HEADS_B, t_new, 1), F32),
                        pltpu.VMEM((N_HEADS_B, t_new, HEAD_DIM), F32)],
        input_output_aliases=aliases,
        compiler_params=_params("parallel", "arbitrary"),
        name="attn_sample",
    )(*args)


def kernel(x_prompt, x_sample, state_mlstm_C, state_mlstm_n, state_mlstm_m, cache_win_k, cache_win_v,
           g_ffn1, w_ffn1_gate, w_ffn1_up, w_ffn1_down, g_mix, w_in, b_igate, b_fgate, g_head, w_out,
           rel_bias, g_ffn2, w_ffn2_gate, w_ffn2_up, w_ffn2_down, g_final):
    bp, sp, d = x_prompt.shape
    bs, ts, _ = x_sample.shape
    depth = g_mix.shape[0]
    wbuf = cache_win_k.shape[2]
    mp, ms = bp * sp, bs * ts
    m_all = mp + ms
    d_ff = w_ffn1_gate.shape[2]
    tm_big = _row_tile(m_all, (1376, 1024, 512, 64))
    tm_down = _row_tile(m_all, (688, 512, 64))

    x = jnp.concatenate([x_prompt.reshape(mp, d), x_sample.reshape(ms, d)], axis=0)
    zc = jnp.zeros((bp, N_HEADS_A, DK_A, DV_A), F32)
    zn = jnp.zeros((bp, N_HEADS_A, DK_A), F32)
    zm = jnp.zeros((bp, N_HEADS_A), F32)
    bias_rows = _prompt_bias_rows(rel_bias)
    bias_c, bias_n, valid_c = _sample_bias(rel_bias, wbuf, ts)
    cache_k = cache_win_k.reshape(depth, bs, wbuf // ts, ts * N_HEADS_B, HEAD_DIM)
    cache_v = cache_win_v.reshape(depth, bs, wbuf // ts, ts * N_HEADS_B, HEAD_DIM)
    w_in_t = jnp.swapaxes(w_in, 1, 2)

    def ffn(x, xg, ssq, wg, wu, wd, l, next_gain):
        a, wd16 = _ffn_up(xg, ssq, wg, wu, wd, l, tm_big)
        w_spec = pl.BlockSpec((d_ff, 256), lambda i, j: (0, j))
        return _matmul_res([a], [w_spec], [wd16], x, 0.5, tm_down, 256, next_gain)

    p_c, p_n, p_m, s_c, s_n, s_m = [], [], [], [], [], []
    pk = pv = sk = sv = None
    xg, ssq = _gain_ssq(x, g_ffn1[0])
    for l in range(depth):
        x, xg, ssq = ffn(x, xg, ssq, w_ffn1_gate, w_ffn1_up, w_ffn1_down, l, g_mix[l])

        pa = _proj_nt(xg, ssq, w_in_t, l, 0, A_WIDTH, 512, tm_big)
        gates = _proj_nt(xg, ssq, w_in_t, l, A_WIDTH // LANES, LANES, LANES, tm_big)
        gate_bias = jnp.pad(jnp.concatenate([b_igate[l], b_fgate[l]]),
                            (0, LANES - GATE_COLS)).reshape(1, LANES)

        ya, c_, n_, m_ = _mlstm(pa, gates, gate_bias, g_head, l, zc, zn, zm,
                                0, bp, sp, PROMPT_CHUNK, BF16, out_rows=m_all)
        p_c.append(c_); p_n.append(n_); p_m.append(m_)
        ya_s, c_, n_, m_ = _mlstm(pa, gates, gate_bias, g_head, l, state_mlstm_C[l], state_mlstm_n[l],
                                  state_mlstm_m[l], mp, bs, ts, ts, F32)
        s_c.append(c_); s_n.append(n_); s_m.append(m_)

        q_p, pk, pv = _qkv_prompt(xg, ssq, w_in_t, l, depth, mp, pk, pv)
        yb = _attn_prompt(q_p, pk, pv, l, bias_rows, bp, sp, out_rows=m_all)
        q_s, k_s, v_s = _qkv_sample(xg, ssq, w_in_t, l, mp, ms)
        yb_s, sk, sv = _attn_sample(q_s, k_s, v_s, cache_k, cache_v, l, bias_c, bias_n, valid_c, bs, ts,
                                    sk, sv)

        ya, yb = _place_rows(ya_s, yb_s, ya, yb, mp)
        wo_specs = [pl.BlockSpec((None, MIX_A, 256), lambda i, j, l=l: (l, 0, j)),
                    pl.BlockSpec((None, MIX_B, 256), lambda i, j, l=l: (l, MIX_A // MIX_B, j))]
        x, xg, ssq = _matmul_res([ya, yb], wo_specs, [w_out, w_out], x, 1.0, tm_big, 256, g_ffn2[l])

        if l + 1 < depth:
            x, xg, ssq = ffn(x, xg, ssq, w_ffn2_gate, w_ffn2_up, w_ffn2_down, l, g_ffn1[l + 1])
        else:
            x = ffn(x, xg, ssq, w_ffn2_gate, w_ffn2_up, w_ffn2_down, l, None)

    yp = _rmsnorm(x, g_final, F32, 0, mp).reshape(bp, sp, d)
    ys = _rmsnorm(x, g_final, F32, mp, ms).reshape(bs, ts, d)
    return (yp, ys, jnp.stack(p_c), jnp.stack(p_n), jnp.stack(p_m),
            pk.reshape(depth, bp, sp, N_HEADS_B, HEAD_DIM), pv.reshape(depth, bp, sp, N_HEADS_B, HEAD_DIM),
            jnp.stack(s_c), jnp.stack(s_n), jnp.stack(s_m),
            sk.reshape(depth, bs, wbuf, N_HEADS_B, HEAD_DIM), sv.reshape(depth, bs, wbuf, N_HEADS_B, HEAD_DIM))
```

```python
import functools
import math

import numpy as np
import jax
import jax.numpy as jnp
from jax import lax
from jax.experimental import pallas as pl
from jax.experimental.pallas import tpu as pltpu

F32 = jnp.float32
BF16 = jnp.bfloat16

D_MODEL = 4096
N_HEADS_A = 8
DK_A = 128
DV_A = 256
MIX_A = N_HEADS_A * DV_A
N_HEADS_B = 16
HEAD_DIM = 128
MIX_B = N_HEADS_B * HEAD_DIM
PATTERNS = ((128, 1), (512, 4), (2048, 16))
N_STEPS = 128
N_BUCKETS = 32
MAX_DISTANCE = 2048
EPS = 1e-6
NEG_INF = -1e30
A_WIDTH = 2 * N_HEADS_A * DK_A + 2 * MIX_A
GATE_COLS = 2 * N_HEADS_A
LANES = 128
SUBLANES = 8
PROMPT_CHUNK = 512

MXU_WIDTH = 256
TN = MXU_WIDTH
TN_WIDE = 2 * MXU_WIDTH
ROW_TILES_WIDE_K = (1376, 1024, 512, 64)
ROW_TILES_FF_K = (688, 512, 64)
ROW_TILES_NORM = (256, 192, 128, 64)
QKV_ROW_TILE = 1024
WD_CAST_ROWS = 64
CACHE_CHUNKS = 4
MERGE_ROWS = 256

VMEM_LIMIT = 56 * 1024 * 1024


def _params(*sem):
    return pltpu.CompilerParams(dimension_semantics=sem, vmem_limit_bytes=VMEM_LIMIT)


def _row_tile(m, candidates):
    for t in candidates:
        if m % t == 0:
            return t
    raise ValueError(f"no row tile for {m} rows among {candidates}")


def _inv_rms(ssq_ref, d):
    return lax.rsqrt(ssq_ref[:, 0:1] * (1.0 / d) + EPS)


def _rmsnorm_kernel(x_ref, g_ref, o_ref):
    x = x_ref[...]
    ms = jnp.mean(x * x, axis=-1, keepdims=True)
    o_ref[...] = (x * lax.rsqrt(ms + EPS) * g_ref[...]).astype(o_ref.dtype)


def _rmsnorm(x, g, out_dtype, row_start, n_rows):
    _, d = x.shape
    tm = _row_tile(math.gcd(n_rows, row_start) if row_start else n_rows, ROW_TILES_NORM)
    off = row_start // tm
    return pl.pallas_call(
        _rmsnorm_kernel,
        grid=(n_rows // tm,),
        in_specs=[pl.BlockSpec((tm, d), lambda i: (off + i, 0)),
                  pl.BlockSpec((1, d), lambda i: (0, 0))],
        out_specs=pl.BlockSpec((tm, d), lambda i: (i, 0)),
        out_shape=jax.ShapeDtypeStruct((n_rows, d), out_dtype),
        compiler_params=_params("parallel"),
        name="rmsnorm",
    )(x, g.reshape(1, d))


def _gain_ssq_kernel(x_ref, g_ref, xg_ref, ssq_ref):
    x = x_ref[...]
    xg_ref[...] = (x * g_ref[...]).astype(xg_ref.dtype)
    ssq_ref[...] = jnp.broadcast_to(jnp.sum(x * x, axis=-1, keepdims=True), ssq_ref.shape)


def _gain_ssq(x, g):
    m, d = x.shape
    tm = _row_tile(m, ROW_TILES_NORM)
    return pl.pallas_call(
        _gain_ssq_kernel,
        grid=(m // tm,),
        in_specs=[pl.BlockSpec((tm, d), lambda i: (i, 0)),
                  pl.BlockSpec((1, d), lambda i: (0, 0))],
        out_specs=[pl.BlockSpec((tm, d), lambda i: (i, 0)),
                   pl.BlockSpec((tm, LANES), lambda i: (i, 0))],
        out_shape=[jax.ShapeDtypeStruct((m, d), BF16), jax.ShapeDtypeStruct((m, LANES), F32)],
        compiler_params=_params("parallel"),
        name="gain_ssq",
    )(x, g.reshape(1, d))


_NT = (((1,), (1,)), ((), ()))


def _row_chunks(ref, n):
    rows = ref.shape[0] // n
    return [slice(c * rows, (c + 1) * rows) for c in range(n)]


def _proj_nt_kernel(x_ref, ssq_ref, w_ref, *refs, row_chunks):
    o_ref = refs[-1]
    w = w_ref[...].astype(BF16)
    for rs in _row_chunks(x_ref, row_chunks):
        acc = lax.dot_general(x_ref[rs, :], w, _NT, preferred_element_type=F32)
        o_ref[rs, :] = acc * lax.rsqrt(ssq_ref[rs, 0:1] * (1.0 / x_ref.shape[1]) + EPS)


def _after(in_specs, args, after):
    if after is not None:
        in_specs.append(pl.BlockSpec(memory_space=pl.ANY))
        args.append(after)


def _proj_nt(xg, ssq, wt, layer, row_block0, n_cols, tn, tm, after=None):
    m, k = xg.shape
    in_specs = [pl.BlockSpec((tm, k), lambda i, j: (i, 0)),
                pl.BlockSpec((tm, LANES), lambda i, j: (i, 0)),
                pl.BlockSpec((None, tn, k), lambda i, j: (layer, row_block0 + j, 0))]
    args = [xg, ssq, wt]
    _after(in_specs, args, after)
    return pl.pallas_call(
        functools.partial(_proj_nt_kernel, row_chunks=2 if tm % 32 == 0 else 1),
        grid=(m // tm, n_cols // tn),
        in_specs=in_specs,
        out_specs=pl.BlockSpec((tm, tn), lambda i, j: (i, j)),
        out_shape=jax.ShapeDtypeStruct((m, n_cols), F32),
        compiler_params=_params("parallel", "arbitrary"),
        name="proj_nt",
    )(*args)


def _qkv_kernel(x_ref, ssq_ref, wq_ref, wk_ref, wv_ref, *refs):
    q_ref, k_ref, v_ref = refs[-3:]
    x = x_ref[...]
    inv = _inv_rms(ssq_ref, x_ref.shape[1])
    for w_ref, o_ref in ((wq_ref, q_ref), (wk_ref, k_ref), (wv_ref, v_ref)):
        acc = lax.dot_general(x, w_ref[...].astype(BF16), _NT, preferred_element_type=F32)
        o_ref[...] = acc * inv


def _qkv_weight_specs(wt2d_rows, layer, tn, k_dim):
    base = layer * wt2d_rows + A_WIDTH + GATE_COLS
    specs = []
    for part in range(3):
        def index(*ids, part=part):
            j = ids[-1]
            return (pl.multiple_of(base + part * MIX_B + j * tn, SUBLANES), 0)
        specs.append(pl.BlockSpec((pl.Element(tn), pl.Element(k_dim)), index))
    return specs


def _qkv_prompt(xg, ssq, wt, layer, depth, n_rows, prev_k, prev_v, tm=QKV_ROW_TILE, tn=TN):
    k_dim = xg.shape[1]
    nb = MIX_B // tn
    n_all = wt.shape[1]
    kv_shape = jax.ShapeDtypeStruct((depth, n_rows, MIX_B), F32)
    in_specs = ([pl.BlockSpec((tm, k_dim), lambda i, j: (i, 0)),
                 pl.BlockSpec((tm, LANES), lambda i, j: (i, 0))]
                + _qkv_weight_specs(n_all, layer, tn, k_dim))
    wt2d = wt.reshape(depth * n_all, k_dim)
    args = [xg, ssq, wt2d, wt2d, wt2d]
    aliases = {}
    if prev_k is not None:
        in_specs += [pl.BlockSpec(memory_space=pl.ANY), pl.BlockSpec(memory_space=pl.ANY)]
        args += [prev_k, prev_v]
        aliases = {5: 1, 6: 2}
    return pl.pallas_call(
        _qkv_kernel,
        grid=(n_rows // tm, nb),
        in_specs=in_specs,
        out_specs=[pl.BlockSpec((tm, tn), lambda i, j: (i, j)),
                   pl.BlockSpec((None, tm, tn), lambda i, j: (layer, i, j)),
                   pl.BlockSpec((None, tm, tn), lambda i, j: (layer, i, j))],
        out_shape=[jax.ShapeDtypeStruct((n_rows, MIX_B), F32), kv_shape, kv_shape],
        input_output_aliases=aliases,
        compiler_params=_params("parallel", "arbitrary"),
        name="qkv_prompt",
    )(*args)


def _qkv_sample(xg, ssq, wt, layer, row_start, n_rows, tn=TN, after=None):
    k_dim = xg.shape[1]
    nb = MIX_B // tn
    depth, n_all, _ = wt.shape
    off = row_start // n_rows
    out = jax.ShapeDtypeStruct((n_rows, MIX_B), F32)
    wt2d = wt.reshape(depth * n_all, k_dim)
    in_specs = ([pl.BlockSpec((n_rows, k_dim), lambda j: (off, 0)),
                 pl.BlockSpec((n_rows, LANES), lambda j: (off, 0))]
                + _qkv_weight_specs(n_all, layer, tn, k_dim))
    args = [xg, ssq, wt2d, wt2d, wt2d]
    _after(in_specs, args, after)
    return pl.pallas_call(
        _qkv_kernel,
        grid=(nb,),
        in_specs=in_specs,
        out_specs=[pl.BlockSpec((n_rows, tn), lambda j: (0, j))] * 3,
        out_shape=[out, out, out],
        compiler_params=_params("arbitrary"),
        name="qkv_sample",
    )(*args)


def _ffn_up_kernel(x_ref, ssq_ref, wg_ref, wu_ref, wd_ref, o_ref, wd16_ref, *, row_chunks):
    wg = wg_ref[...].astype(BF16)
    wu = wu_ref[...].astype(BF16)
    for rs in _row_chunks(x_ref, row_chunks):
        x = x_ref[rs, :]
        inv = lax.rsqrt(ssq_ref[rs, 0:1] * (1.0 / x_ref.shape[1]) + EPS)
        g = jnp.dot(x, wg, preferred_element_type=F32) * inv
        u = jnp.dot(x, wu, preferred_element_type=F32) * inv
        o_ref[rs, :] = (g * jax.nn.sigmoid(g) * u).astype(o_ref.dtype)
    wd16_ref[...] = wd_ref[...].astype(wd16_ref.dtype)


def _ffn_up(xg, ssq, wg, wu, wd, layer, tm, tn=TN, wd_rows=WD_CAST_ROWS, row_chunks=2):
    m, k = xg.shape
    n = wg.shape[2]
    n_j = n // tn
    n_slabs = wd.shape[1] // wd_rows
    assert wd.shape[1] % wd_rows == 0 and n_slabs <= (m // tm) * n_j

    def slab(i, j):
        return jnp.minimum(i * n_j + j, n_slabs - 1)

    return pl.pallas_call(
        functools.partial(_ffn_up_kernel, row_chunks=row_chunks),
        grid=(m // tm, n_j),
        in_specs=[pl.BlockSpec((tm, k), lambda i, j: (i, 0)),
                  pl.BlockSpec((tm, LANES), lambda i, j: (i, 0)),
                  pl.BlockSpec((None, k, tn), lambda i, j: (layer, 0, j)),
                  pl.BlockSpec((None, k, tn), lambda i, j: (layer, 0, j)),
                  pl.BlockSpec((None, wd_rows, wd.shape[2]), lambda i, j: (layer, slab(i, j), 0))],
        out_specs=[pl.BlockSpec((tm, tn), lambda i, j: (i, j)),
                   pl.BlockSpec((wd_rows, wd.shape[2]), lambda i, j: (slab(i, j), 0))],
        out_shape=[jax.ShapeDtypeStruct((m, n), BF16),
                   jax.ShapeDtypeStruct(wd.shape[1:], BF16)],
        compiler_params=_params("arbitrary", "arbitrary"),
        name="ffn_up",
    )(xg, ssq, wg, wu, wd)


def _matmul_res_kernel(*refs, scale, emit_norm, n_lhs):
    a_refs, w_refs, (r_ref, *refs) = refs[:n_lhs], refs[n_lhs:2 * n_lhs], refs[2 * n_lhs:]
    acc = None
    for a_ref, w_ref in zip(a_refs, w_refs):
        prod = jnp.dot(a_ref[...], w_ref[...].astype(BF16), preferred_element_type=F32)
        acc = prod if acc is None else acc + prod
    x_new = r_ref[...] + scale * acc
    if not emit_norm:
        (o_ref,) = refs
        o_ref[...] = x_new
        return
    g_ref, o_ref, xg_ref, ssq_ref = refs
    o_ref[...] = x_new
    xg_ref[...] = (x_new * g_ref[...]).astype(xg_ref.dtype)
    part = jnp.broadcast_to(jnp.sum(x_new * x_new, axis=-1, keepdims=True), ssq_ref.shape)

    @pl.when(pl.program_id(1) == 0)
    def _():
        ssq_ref[...] = part

    @pl.when(pl.program_id(1) > 0)
    def _():
        ssq_ref[...] += part


def _matmul_res(lhs, w_specs, ws, res, scale, tm, tn, next_gain=None):
    m = lhs[0].shape[0]
    n = res.shape[1]
    emit_norm = next_gain is not None
    in_specs = ([pl.BlockSpec((tm, a.shape[1]), lambda i, j: (i, 0)) for a in lhs] + list(w_specs)
                + [pl.BlockSpec((tm, tn), lambda i, j: (i, j))])
    args = list(lhs) + list(ws) + [res]
    out_specs = [pl.BlockSpec((tm, tn), lambda i, j: (i, j))]
    out_shape = [jax.ShapeDtypeStruct((m, n), F32)]
    if emit_norm:
        in_specs.append(pl.BlockSpec((1, tn), lambda i, j: (0, j)))
        args.append(next_gain.reshape(1, n))
        out_specs += [pl.BlockSpec((tm, tn), lambda i, j: (i, j)),
                      pl.BlockSpec((tm, LANES), lambda i, j: (i, 0))]
        out_shape += [jax.ShapeDtypeStruct((m, n), BF16), jax.ShapeDtypeStruct((m, LANES), F32)]
    out = pl.pallas_call(
        functools.partial(_matmul_res_kernel, scale=scale, emit_norm=emit_norm, n_lhs=len(lhs)),
        grid=(m // tm, n // tn),
        in_specs=in_specs,
        out_specs=out_specs,
        out_shape=out_shape,
        compiler_params=_params("parallel", "arbitrary"),
        name="matmul_res",
    )(*args)
    return out if emit_norm else out[0]


def _place_rows_kernel(a_ref, b_ref, ya_in, yb_in, ya_ref, yb_ref):
    del ya_in, yb_in
    ya_ref[...] = a_ref[...].astype(ya_ref.dtype)
    yb_ref[...] = b_ref[...].astype(yb_ref.dtype)


def _place_rows(a, b, ya, yb, row_start):
    rows = a.shape[0]
    blk = row_start // rows
    return pl.pallas_call(
        _place_rows_kernel,
        grid=(1,),
        in_specs=[pl.BlockSpec(a.shape, lambda i: (0, 0)), pl.BlockSpec(b.shape, lambda i: (0, 0)),
                  pl.BlockSpec(memory_space=pl.ANY), pl.BlockSpec(memory_space=pl.ANY)],
        out_specs=[pl.BlockSpec(a.shape, lambda i: (blk, 0)), pl.BlockSpec(b.shape, lambda i: (blk, 0))],
        out_shape=[jax.ShapeDtypeStruct(ya.shape, ya.dtype), jax.ShapeDtypeStruct(yb.shape, yb.dtype)],
        input_output_aliases={2: 0, 3: 1},
        compiler_params=_params("arbitrary"),
        name="place_rows",
    )(a, b, ya, yb)


def _mlstm_kernel(q_ref, k_ref, v_ref, o_ref, g_ref, gb_ref, gh_ref, c0_ref, n0_ref, m0_ref,
                  y_ref, c_out_ref, n_out_ref, m_out_ref, c_scr, n_scr, m_scr, *, chunk, n_chunks):
    c_idx = pl.program_id(1)
    L = chunk

    @pl.when(c_idx == 0)
    def _():
        c_scr[...] = c0_ref[0]
        n_scr[...] = n0_ref[0]
        m_scr[...] = m0_ref[0]

    gates = g_ref[...] + gb_ref[...]
    lane = lax.broadcasted_iota(jnp.int32, gates.shape, 1)
    row = lax.broadcasted_iota(jnp.int32, (L, L), 0)
    col = lax.broadcasted_iota(jnp.int32, (L, L), 1)
    eye = row == col
    causal = col <= row
    anti = row <= col

    for head in range(N_HEADS_A):
        qs = slice(head * DK_A, (head + 1) * DK_A)
        vs = slice(head * DV_A, (head + 1) * DV_A)
        i_col = jnp.sum(jnp.where(lane == head, gates, 0.0), axis=1, keepdims=True)
        f_col = jnp.sum(jnp.where(lane == head + N_HEADS_A, gates, 0.0), axis=1, keepdims=True)
        lf_col = jax.nn.log_sigmoid(f_col)
        lf_row = jnp.sum(jnp.where(eye, lf_col, 0.0), axis=0, keepdims=True)
        i_row = jnp.sum(jnp.where(eye, i_col, 0.0), axis=0, keepdims=True)
        b_col = jnp.sum(jnp.where(causal, lf_row, 0.0), axis=1, keepdims=True)
        b_row = jnp.sum(jnp.where(anti, lf_col, 0.0), axis=0, keepdims=True)

        m_prev = m_scr[head]
        dmat = jnp.where(causal, b_col - b_row + i_row, -jnp.inf)
        inter = b_col + m_prev
        mt = jnp.maximum(inter, jnp.max(dmat, axis=1, keepdims=True))

        q32 = q_ref[:, qs]
        k32 = k_ref[:, qs] * (DK_A ** -0.5)
        q = q32.astype(BF16)
        k = k32.astype(BF16)
        v = v_ref[:, vs].astype(BF16)
        c_prev = c_scr[head]
        n_prev = n_scr[head]

        s = lax.dot_general(q, k, _NT, preferred_element_type=F32)
        w = s * jnp.exp(dmat - mt)
        a = jnp.exp(inter - mt)
        num = (jnp.dot(w.astype(BF16), v, preferred_element_type=F32)
               + a * jnp.dot(q, c_prev.astype(BF16), preferred_element_type=F32))
        den = (jnp.sum(w, axis=1, keepdims=True)
               + a * jnp.sum(q32 * n_prev, axis=1, keepdims=True))
        h = num / jnp.maximum(jnp.abs(den), jnp.exp(-mt))

        m_new = mt[L - 1:L, :]
        b_last = b_col[L - 1:L, :]
        g = jnp.exp(b_last - b_col + i_col - m_new)
        dec = jnp.exp(b_last + m_prev - m_new)
        gk = g * k32
        c_new = dec * c_prev + lax.dot_general(gk.astype(BF16), v, (((0,), (0,)), ((), ())),
                                               preferred_element_type=F32)
        n_new = dec * n_prev + jnp.sum(gk, axis=0, keepdims=True)
        c_scr[head] = c_new
        n_scr[head] = n_new
        m_scr[head] = m_new

        hn = h * lax.rsqrt(jnp.mean(h * h, axis=-1, keepdims=True) + EPS)
        y_ref[:, vs] = (hn * gh_ref[:, vs] * jax.nn.sigmoid(o_ref[:, vs])).astype(y_ref.dtype)

    @pl.when(c_idx == n_chunks - 1)
    def _():
        c_out_ref[0] = c_scr[...]
        n_out_ref[0] = n_scr[...]
        m_out_ref[0] = m_scr[...]


def _mlstm(pa, gates, gate_bias, g_head, layer, c0, n0, m0, row_start, batch, seq, chunk, out_dtype,
           out_rows=None):
    n_chunks = seq // chunk
    m_rows = batch * seq
    qk_w = N_HEADS_A * DK_A
    r0 = row_start // chunk
    h_a = N_HEADS_A

    def rows(b, c):
        return r0 + b * n_chunks + c

    def state_specs():
        return [pl.BlockSpec((1, h_a, DK_A, DV_A), lambda b, c: (b, 0, 0, 0)),
                pl.BlockSpec((1, h_a, 1, DK_A), lambda b, c: (b, 0, 0, 0)),
                pl.BlockSpec((1, h_a, 1, 1), lambda b, c: (b, 0, 0, 0))]

    kernel = functools.partial(_mlstm_kernel, chunk=chunk, n_chunks=n_chunks)
    y, c_fin, n_fin, m_fin = pl.pallas_call(
        kernel,
        grid=(batch, n_chunks),
        in_specs=[
            pl.BlockSpec((chunk, qk_w), lambda b, c: (rows(b, c), 0)),
            pl.BlockSpec((chunk, qk_w), lambda b, c: (rows(b, c), 1)),
            pl.BlockSpec((chunk, MIX_A), lambda b, c: (rows(b, c), 2 * qk_w // MIX_A)),
            pl.BlockSpec((chunk, MIX_A), lambda b, c: (rows(b, c), 2 * qk_w // MIX_A + 1)),
            pl.BlockSpec((chunk, LANES), lambda b, c: (rows(b, c), 0)),
            pl.BlockSpec((1, LANES), lambda b, c: (0, 0)),
            pl.BlockSpec((None, 1, MIX_A), lambda b, c: (layer, 0, 0)),
        ] + state_specs(),
        out_specs=[pl.BlockSpec((chunk, MIX_A), lambda b, c: (b * n_chunks + c, 0))] + state_specs(),
        out_shape=[
            jax.ShapeDtypeStruct((out_rows or m_rows, MIX_A), out_dtype),
            jax.ShapeDtypeStruct((batch, h_a, DK_A, DV_A), F32),
            jax.ShapeDtypeStruct((batch, h_a, 1, DK_A), F32),
            jax.ShapeDtypeStruct((batch, h_a, 1, 1), F32),
        ],
        scratch_shapes=[pltpu.VMEM((h_a, DK_A, DV_A), F32),
                        pltpu.VMEM((h_a, 1, DK_A), F32),
                        pltpu.VMEM((h_a, 1, 1), F32)],
        compiler_params=_params("parallel", "arbitrary"),
        name="mlstm",
    )(pa, pa, pa, pa, gates, gate_bias, g_head.reshape(g_head.shape[0], 1, MIX_A),
      c0, n0.reshape(batch, h_a, 1, DK_A), m0.reshape(batch, h_a, 1, 1))
    return (y, c_fin, n_fin.reshape(batch, h_a, DK_A), m_fin.reshape(batch, h_a))


def _t5_bucket(dist):
    max_exact = N_BUCKETS // 2
    d = np.maximum(dist, 1).astype(np.float64)
    large = max_exact + (np.log(d / max_exact) / math.log(MAX_DISTANCE / max_exact)
                         * (N_BUCKETS - max_exact)).astype(np.int32)
    large = np.minimum(large, N_BUCKETS - 1)
    return np.where(dist < max_exact, dist, large).astype(np.int32)


def _prompt_bias_rows(rel_bias):
    j = np.arange(2 * N_STEPS)
    valid = j <= N_STEPS
    rows = []
    for (_, r) in PATTERNS:
        bucket = _t5_bucket(np.clip(N_STEPS - j, 0, N_STEPS) * r)
        rows.append(jnp.where(valid[:, None], rel_bias[bucket].astype(F32), NEG_INF))
    return jnp.stack(rows, 0).transpose(2, 0, 1)[:, :, None, :]


def _attn_prompt_kernel(q_ref, k_ref, v_ref, bias_ref, y_ref, acc_scr, m_scr, l_scr, *, seq):
    scale = HEAD_DIM ** -0.5
    nw = N_STEPS
    n_blocks = seq // nw

    for p, (_, r) in enumerate(PATTERNS):
        nb = n_blocks // r

        def rows(g):
            c, n = divmod(g, nb)
            start = c + n * nw * r
            return pl.ds(start, nw, stride=r) if r > 1 else pl.ds(start, nw)

        tile = pltpu.roll(jnp.broadcast_to(bias_ref[0, p], (nw, 2 * nw)), 0, 1,
                          stride=1, stride_axis=0)
        tile_cur = tile[:, nw:]
        qs = [q_ref[rows(g), :].astype(BF16) for g in range(n_blocks)]
        ks = [k_ref[rows(g), :].astype(BF16) for g in range(n_blocks)]
        vs = [v_ref[rows(g), :].astype(BF16) for g in range(n_blocks)]
        for g in range(n_blocks):
            if g % nb:
                kk = jnp.concatenate([ks[g - 1], ks[g]], axis=0)
                vv = jnp.concatenate([vs[g - 1], vs[g]], axis=0)
                bias = tile
            else:
                kk, vv, bias = ks[g], vs[g], tile_cur
            s = lax.dot_general(qs[g], kk, _NT, preferred_element_type=F32)
            s = s * scale + bias
            m = jnp.max(s, axis=1, keepdims=True)
            e = jnp.exp(s - m)
            l = jnp.sum(e, axis=1, keepdims=True)
            acc = jnp.dot(e.astype(BF16), vv, preferred_element_type=F32)
            acc_scr[p, rows(g), :] = acc
            m_scr[p, rows(g), :] = jnp.broadcast_to(m, (nw, HEAD_DIM))
            l_scr[p, rows(g), :] = jnp.broadcast_to(l, (nw, HEAD_DIM))

    rows_per_step = MERGE_ROWS

    def merge(t, carry):
        rws = pl.ds(pl.multiple_of(t * rows_per_step, rows_per_step), rows_per_step)
        m0, m1, m2 = m_scr[0, rws, :], m_scr[1, rws, :], m_scr[2, rws, :]
        mx = jnp.maximum(jnp.maximum(m0, m1), m2)
        w0, w1, w2 = jnp.exp(m0 - mx), jnp.exp(m1 - mx), jnp.exp(m2 - mx)
        num = w0 * acc_scr[0, rws, :] + w1 * acc_scr[1, rws, :] + w2 * acc_scr[2, rws, :]
        den = w0 * l_scr[0, rws, :] + w1 * l_scr[1, rws, :] + w2 * l_scr[2, rws, :]
        y_ref[rws, :] = (num / den).astype(y_ref.dtype)
        return carry
    lax.fori_loop(0, seq // rows_per_step, merge, 0)


def _attn_prompt(q, k_all, v_all, layer, bias_rows, batch, seq, out_rows=None):
    n_pat = len(PATTERNS)
    kernel = functools.partial(_attn_prompt_kernel, seq=seq)
    return pl.pallas_call(
        kernel,
        grid=(batch, N_HEADS_B),
        in_specs=[
            pl.BlockSpec((seq, HEAD_DIM), lambda b, h: (b, h)),
            pl.BlockSpec((None, seq, HEAD_DIM), lambda b, h: (layer, b, h)),
            pl.BlockSpec((None, seq, HEAD_DIM), lambda b, h: (layer, b, h)),
            pl.BlockSpec((1, n_pat, 1, 2 * N_STEPS), lambda b, h: (h, 0, 0, 0)),
        ],
        out_specs=pl.BlockSpec((seq, HEAD_DIM), lambda b, h: (b, h)),
        out_shape=jax.ShapeDtypeStruct((out_rows or batch * seq, MIX_B), BF16),
        scratch_shapes=[pltpu.VMEM((n_pat, seq, HEAD_DIM), F32),
                        pltpu.VMEM((n_pat, seq, HEAD_DIM), F32),
                        pltpu.VMEM((n_pat, seq, HEAD_DIM), F32)],
        compiler_params=_params("parallel", "parallel"),
        name="attn_prompt",
    )(q, k_all, v_all, bias_rows)


def _sample_bias(rel_bias, wbuf, t_new):
    n = wbuf + t_new
    delta = np.arange(n)
    rows, valid_by_dist = [], []
    for (w, r) in PATTERNS:
        valid = (delta % r == 0) & (delta // r <= w // r)
        valid_by_dist.append(valid)
        b = rel_bias[_t5_bucket(np.minimum(delta, w))].astype(F32)
        rows.append(jnp.where(valid[:, None], b, NEG_INF))
    f = jnp.stack(rows, 0).transpose(2, 0, 1)
    f_rev = jnp.pad(f[:, :, ::-1], ((0, 0), (0, 0), (0, t_new)), constant_values=NEG_INF)
    bias_c = jnp.stack([f_rev[:, :, t_new - 1 - t: t_new - 1 - t + wbuf] for t in range(t_new)], axis=2)
    bias_n = jnp.stack([f_rev[:, :, n - 1 - t: n - 1 - t + t_new] for t in range(t_new)], axis=2)
    dist = np.arange(t_new)[:, None] + wbuf - np.arange(wbuf)[None, :]
    valid_c = np.stack([v[dist] for v in valid_by_dist], 0)
    return bias_c, bias_n, valid_c


def _far_chunks(valid_c, n_chunks, t_new):
    r_max = max(r for _, r in PATTERNS)
    wbuf = valid_c.shape[2]
    tokens = wbuf // n_chunks
    if r_max % t_new or tokens % r_max:
        return ()
    far = []
    for c in range(n_chunks):
        in_chunk = valid_c[:, :, c * tokens:(c + 1) * tokens]
        only_widest = all(not in_chunk[p].any() for p, (_, r) in enumerate(PATTERNS) if r != r_max)
        tok = np.arange(c * tokens, (c + 1) * tokens)
        inside = not in_chunk[:, :, (tok % r_max) >= t_new].any()
        if only_widest and inside:
            far.append(c)
    return tuple(far)


def _attn_sample_kernel(q_ref, kn_ref, vn_ref, kc_ref, vc_ref, kx_ref, vx_ref, bc_ref, bf_ref, bn_ref,
                        *refs, n_chunks, far_chunks, group_step):
    y_ref, ko_ref, vo_ref, m_scr, l_scr, acc_scr = refs[-6:]
    scale = HEAD_DIM ** -0.5
    n_pat = len(PATTERNS)
    chunk = pl.program_id(1)
    t_new = q_ref.shape[0]
    groups = kc_ref.shape[0]

    ko_ref[0:groups - 1] = kc_ref[1:groups]
    vo_ref[0:groups - 1] = vc_ref[1:groups]

    @pl.when(chunk < n_chunks - 1)
    def _():
        ko_ref[groups - 1] = kx_ref[0]
        vo_ref[groups - 1] = vx_ref[0]

    @pl.when(chunk == n_chunks - 1)
    def _():
        for h in range(N_HEADS_B):
            dst = pl.ds(h, t_new, stride=N_HEADS_B)
            ko_ref[groups - 1, dst, :] = kn_ref[:, h * HEAD_DIM:(h + 1) * HEAD_DIM]
            vo_ref[groups - 1, dst, :] = vn_ref[:, h * HEAD_DIM:(h + 1) * HEAD_DIM]

    @pl.when(chunk == 0)
    def _():
        m_scr[...] = jnp.full(m_scr.shape, NEG_INF, F32)
        l_scr[...] = jnp.zeros(l_scr.shape, F32)
        acc_scr[...] = jnp.zeros(acc_scr.shape, F32)

    def update(h, q, k, v, biases):
        s = lax.dot_general(q, k, _NT, preferred_element_type=F32) * scale
        zs = [s + b for b in biases]
        zmax = jnp.max(zs[0], axis=1, keepdims=True)
        for z in zs[1:]:
            zmax = jnp.maximum(zmax, jnp.max(z, axis=1, keepdims=True))
        m_old = m_scr[h]
        m_new = jnp.maximum(m_old, zmax)
        alpha = jnp.exp(m_old - m_new)
        pt = jnp.exp(zs[0] - m_new)
        for z in zs[1:]:
            pt = pt + jnp.exp(z - m_new)
        m_scr[h] = m_new
        l_scr[h] = alpha * l_scr[h] + jnp.sum(pt, axis=1, keepdims=True)
        acc_scr[h] = alpha * acc_scr[h] + jnp.dot(pt.astype(BF16), v, preferred_element_type=F32)

    def head_rows(ref, h, sparse):
        in_group = pl.ds(h, t_new, stride=N_HEADS_B)
        if sparse:
            x = ref[pl.ds(0, groups // group_step, stride=group_step), in_group, :]
        else:
            x = ref[:, in_group, :]
        return x.reshape(-1, HEAD_DIM).astype(BF16)

    def cached(sparse):
        for h in range(N_HEADS_B):
            q = q_ref[:, h * HEAD_DIM:(h + 1) * HEAD_DIM].astype(BF16)
            biases = [bf_ref[h]] if sparse else [bc_ref[h, p] for p in range(n_pat)]
            update(h, q, head_rows(kc_ref, h, sparse), head_rows(vc_ref, h, sparse), biases)

    if far_chunks:
        is_far = functools.reduce(jnp.logical_or, [chunk == c for c in far_chunks])
        pl.when(is_far)(lambda: cached(True))
        pl.when(jnp.logical_not(is_far))(lambda: cached(False))
    else:
        cached(False)

    @pl.when(chunk == n_chunks - 1)
    def _():
        for h in range(N_HEADS_B):
            cols = slice(h * HEAD_DIM, (h + 1) * HEAD_DIM)
            q = q_ref[:, cols].astype(BF16)
            update(h, q, kn_ref[:, cols].astype(BF16), vn_ref[:, cols].astype(BF16),
                   [bn_ref[h, p] for p in range(n_pat)])
            y_ref[:, cols] = (acc_scr[h] / l_scr[h]).astype(y_ref.dtype)


def _attn_sample(q, kn, vn, cache_k, cache_v, layer, bias_c, bias_n, valid_c, batch, t_new,
                 prev_k, prev_v, n_chunks=CACHE_CHUNKS):
    depth, _, all_groups, group_rows, _ = cache_k.shape
    groups = all_groups // n_chunks
    tokens = groups * t_new
    n_pat = len(PATTERNS)
    r_max = max(r for _, r in PATTERNS)
    far_pattern = [r for _, r in PATTERNS].index(r_max)
    far_chunks = _far_chunks(valid_c, n_chunks, t_new)
    group_step = r_max // t_new if far_chunks else 1
    n_heads, _, _, wbuf = bias_c.shape
    bias_far = bias_c[:, far_pattern].reshape(n_heads, t_new, wbuf // r_max, r_max)[..., :t_new]
    bias_far = bias_far.reshape(n_heads, t_new, wbuf // r_max * t_new)
    far_tokens = bias_far.shape[2] // n_chunks

    chunk_spec = pl.BlockSpec((None, None, groups, group_rows, HEAD_DIM), lambda b, c: (layer, b, c, 0, 0))
    next_spec = pl.BlockSpec((None, None, 1, group_rows, HEAD_DIM),
                             lambda b, c: (layer, b, jnp.minimum(c + 1, n_chunks - 1) * groups, 0, 0))
    row_spec = pl.BlockSpec((t_new, MIX_B), lambda b, c: (b, 0))
    in_specs = [row_spec, row_spec, row_spec, chunk_spec, chunk_spec, next_spec, next_spec,
                pl.BlockSpec((N_HEADS_B, n_pat, t_new, tokens), lambda b, c: (0, 0, 0, c)),
                pl.BlockSpec((N_HEADS_B, t_new, far_tokens), lambda b, c: (0, 0, c)),
                pl.BlockSpec((N_HEADS_B, n_pat, t_new, t_new), lambda b, c: (0, 0, 0, 0))]
    args = [q, kn, vn, cache_k, cache_v, cache_k, cache_v, bias_c, bias_far, bias_n]
    aliases = {}
    if prev_k is not None:
        in_specs += [pl.BlockSpec(memory_space=pl.ANY), pl.BlockSpec(memory_space=pl.ANY)]
        args += [prev_k, prev_v]
        aliases = {10: 1, 11: 2}
    cache_shape = jax.ShapeDtypeStruct(cache_k.shape, F32)
    kernel = functools.partial(_attn_sample_kernel, n_chunks=n_chunks, far_chunks=far_chunks,
                               group_step=group_step)
    return pl.pallas_call(
        kernel,
        grid=(batch, n_chunks),
        in_specs=in_specs,
        out_specs=[row_spec, chunk_spec, chunk_spec],
        out_shape=[jax.ShapeDtypeStruct((batch * t_new, MIX_B), F32), cache_shape, cache_shape],
        scratch_shapes=[pltpu.VMEM((N_HEADS_B, t_new, 1), F32),
                        pltpu.VMEM((N_HEADS_B, t_new, 1), F32),
                        pltpu.VMEM((N_HEADS_B, t_new, HEAD_DIM), F32)],
        input_output_aliases=aliases,
        compiler_params=_params("parallel", "arbitrary"),
        name="attn_sample",
    )(*args)


def kernel(x_prompt, x_sample, state_mlstm_C, state_mlstm_n, state_mlstm_m, cache_win_k, cache_win_v,
           g_ffn1, w_ffn1_gate, w_ffn1_up, w_ffn1_down, g_mix, w_in, b_igate, b_fgate, g_head, w_out,
           rel_bias, g_ffn2, w_ffn2_gate, w_ffn2_up, w_ffn2_down, g_final):
    bp, sp, d = x_prompt.shape
    bs, ts, _ = x_sample.shape
    depth = g_mix.shape[0]
    wbuf = cache_win_k.shape[2]
    mp, ms = bp * sp, bs * ts
    m_all = mp + ms
    d_ff = w_ffn1_gate.shape[2]
    tm_big = _row_tile(m_all, ROW_TILES_WIDE_K)
    tm_down = _row_tile(m_all, ROW_TILES_FF_K)

    x = jnp.concatenate([x_prompt.reshape(mp, d), x_sample.reshape(ms, d)], axis=0)
    zc = jnp.zeros((bp, N_HEADS_A, DK_A, DV_A), F32)
    zn = jnp.zeros((bp, N_HEADS_A, DK_A), F32)
    zm = jnp.zeros((bp, N_HEADS_A), F32)
    bias_rows = _prompt_bias_rows(rel_bias)
    bias_c, bias_n, valid_c = _sample_bias(rel_bias, wbuf, ts)
    cache_k = cache_win_k.reshape(depth, bs, wbuf // ts, ts * N_HEADS_B, HEAD_DIM)
    cache_v = cache_win_v.reshape(depth, bs, wbuf // ts, ts * N_HEADS_B, HEAD_DIM)
    w_in_t = jnp.swapaxes(w_in, 1, 2)

    def ffn(x, xg, ssq, wg, wu, wd, l, next_gain):
        a, wd16 = _ffn_up(xg, ssq, wg, wu, wd, l, tm_big)
        w_spec = pl.BlockSpec((d_ff, TN), lambda i, j: (0, j))
        return _matmul_res([a], [w_spec], [wd16], x, 0.5, tm_down, TN, next_gain)

    p_c, p_n, p_m, s_c, s_n, s_m = [], [], [], [], [], []
    pk = pv = sk = sv = None
    xg, ssq = _gain_ssq(x, g_ffn1[0])
    for l in range(depth):
        x, xg, ssq = ffn(x, xg, ssq, w_ffn1_gate, w_ffn1_up, w_ffn1_down, l, g_mix[l])

        q_p, pk, pv = _qkv_prompt(xg, ssq, w_in_t, l, depth, mp, pk, pv)
        yb = _attn_prompt(q_p, pk, pv, l, bias_rows, bp, sp, out_rows=m_all)
        q_s, k_s, v_s = _qkv_sample(xg, ssq, w_in_t, l, mp, ms, after=yb)
        yb_s, sk, sv = _attn_sample(q_s, k_s, v_s, cache_k, cache_v, l, bias_c, bias_n, valid_c, bs, ts,
                                    sk, sv)

        pa = _proj_nt(xg, ssq, w_in_t, l, 0, A_WIDTH, TN_WIDE, tm_big, after=yb)
        gates = _proj_nt(xg, ssq, w_in_t, l, A_WIDTH // LANES, LANES, LANES, tm_big, after=yb)
        gate_bias = jnp.pad(jnp.concatenate([b_igate[l], b_fgate[l]]),
                            (0, LANES - GATE_COLS)).reshape(1, LANES)

        ya, c_, n_, m_ = _mlstm(pa, gates, gate_bias, g_head, l, zc, zn, zm,
                                0, bp, sp, PROMPT_CHUNK, BF16, out_rows=m_all)
        p_c.append(c_); p_n.append(n_); p_m.append(m_)
        ya_s, c_, n_, m_ = _mlstm(pa, gates, gate_bias, g_head, l, state_mlstm_C[l], state_mlstm_n[l],
                                  state_mlstm_m[l], mp, bs, ts, ts, F32)
        s_c.append(c_); s_n.append(n_); s_m.append(m_)

        ya, yb = _place_rows(ya_s, yb_s, ya, yb, mp)
        wo_specs = [pl.BlockSpec((None, MIX_A, TN), lambda i, j, l=l: (l, 0, j)),
                    pl.BlockSpec((None, MIX_B, TN), lambda i, j, l=l: (l, MIX_A // MIX_B, j))]
        x, xg, ssq = _matmul_res([ya, yb], wo_specs, [w_out, w_out], x, 1.0, tm_big, TN, g_ffn2[l])

        if l + 1 < depth:
            x, xg, ssq = ffn(x, xg, ssq, w_ffn2_gate, w_ffn2_up, w_ffn2_down, l, g_ffn1[l + 1])
        else:
            x = ffn(x, xg, ssq, w_ffn2_gate, w_ffn2_up, w_ffn2_down, l, None)

    yp = _rmsnorm(x, g_final, F32, 0, mp).reshape(bp, sp, d)
    ys = _rmsnorm(x, g_final, F32, mp, ms).reshape(bs, ts, d)
    return (yp, ys, jnp.stack(p_c), jnp.stack(p_n), jnp.stack(p_m),
            pk.reshape(depth, bp, sp, N_HEADS_B, HEAD_DIM), pv.reshape(depth, bp, sp, N_HEADS_B, HEAD_DIM),
            jnp.stack(s_c), jnp.stack(s_n), jnp.stack(s_m),
            sk.reshape(depth, bs, wbuf, N_HEADS_B, HEAD_DIM), sv.reshape(depth, bs, wbuf, N_HEADS_B, HEAD_DIM))
```

```python
import functools
import math

import numpy as np
import jax
import jax.numpy as jnp
from jax import lax
from jax.experimental import pallas as pl
from jax.experimental.pallas import tpu as pltpu

F32 = jnp.float32
BF16 = jnp.bfloat16

D_MODEL = 4096
N_HEADS_A = 8
DK_A = 128
DV_A = 256
MIX_A = N_HEADS_A * DV_A
N_HEADS_B = 16
HEAD_DIM = 128
MIX_B = N_HEADS_B * HEAD_DIM
PATTERNS = ((128, 1), (512, 4), (2048, 16))
N_STEPS = 128
N_BUCKETS = 32
MAX_DISTANCE = 2048
EPS = 1e-6
NEG_INF = -1e30
A_WIDTH = 2 * N_HEADS_A * DK_A + 2 * MIX_A
GATE_COLS = 2 * N_HEADS_A
LANES = 128
SUBLANES = 8
PROMPT_CHUNK = 512

MXU_WIDTH = 256
TN = MXU_WIDTH
TN_WIDE = 2 * MXU_WIDTH
ROW_TILES_WIDE_K = (1376, 1024, 512, 64)
ROW_TILES_FF_K = (688, 512, 64)
ROW_TILES_NORM = (256, 192, 128, 64)
QKV_ROW_TILE = 1024
WD_CAST_ROWS = 64
CACHE_CHUNKS = 4
MERGE_ROWS = 256

VMEM_LIMIT = 56 * 1024 * 1024


def _params(*sem):
    return pltpu.CompilerParams(dimension_semantics=sem, vmem_limit_bytes=VMEM_LIMIT)


def _row_tile(m, candidates):
    for t in candidates:
        if m % t == 0:
            return t
    raise ValueError(f"no row tile for {m} rows among {candidates}")


def _inv_rms(ssq_ref, d):
    return lax.rsqrt(ssq_ref[:, 0:1] * (1.0 / d) + EPS)


def _rmsnorm_kernel(x_ref, g_ref, o_ref):
    x = x_ref[...]
    ms = jnp.mean(x * x, axis=-1, keepdims=True)
    o_ref[...] = (x * lax.rsqrt(ms + EPS) * g_ref[...]).astype(o_ref.dtype)


def _rmsnorm(x, g, out_dtype, row_start, n_rows):
    _, d = x.shape
    tm = _row_tile(math.gcd(n_rows, row_start) if row_start else n_rows, ROW_TILES_NORM)
    off = row_start // tm
    return pl.pallas_call(
        _rmsnorm_kernel,
        grid=(n_rows // tm,),
        in_specs=[pl.BlockSpec((tm, d), lambda i: (off + i, 0)),
                  pl.BlockSpec((1, d), lambda i: (0, 0))],
        out_specs=pl.BlockSpec((tm, d), lambda i: (i, 0)),
        out_shape=jax.ShapeDtypeStruct((n_rows, d), out_dtype),
        compiler_params=_params("parallel"),
        name="rmsnorm",
    )(x, g.reshape(1, d))


def _gain_ssq_kernel(x_ref, g_ref, *refs):
    xo_ref, xg_ref, ssq_ref = refs[-3:]
    x = x_ref[...]
    xo_ref[...] = x
    xg_ref[...] = (x * g_ref[...]).astype(xg_ref.dtype)
    ssq_ref[...] = jnp.broadcast_to(jnp.sum(x * x, axis=-1, keepdims=True), ssq_ref.shape)


def _gain_ssq(x, g, row_start, total_rows, prev=None):
    m, d = x.shape
    tm = _row_tile(math.gcd(m, row_start) if row_start else m, ROW_TILES_NORM)
    off = row_start // tm
    in_specs = [pl.BlockSpec((tm, d), lambda i: (i, 0)),
                pl.BlockSpec((1, d), lambda i: (0, 0))]
    args = [x, g.reshape(1, d)]
    aliases = {}
    if prev is not None:
        in_specs += [pl.BlockSpec(memory_space=pl.ANY)] * 3
        args += list(prev)
        aliases = {2: 0, 3: 1, 4: 2}
    return pl.pallas_call(
        _gain_ssq_kernel,
        grid=(m // tm,),
        in_specs=in_specs,
        out_specs=[pl.BlockSpec((tm, d), lambda i: (off + i, 0)),
                   pl.BlockSpec((tm, d), lambda i: (off + i, 0)),
                   pl.BlockSpec((tm, LANES), lambda i: (off + i, 0))],
        out_shape=[jax.ShapeDtypeStruct((total_rows, d), F32),
                   jax.ShapeDtypeStruct((total_rows, d), BF16),
                   jax.ShapeDtypeStruct((total_rows, LANES), F32)],
        input_output_aliases=aliases,
        compiler_params=_params("parallel"),
        name="gain_ssq",
    )(*args)


_NT = (((1,), (1,)), ((), ()))


def _row_chunks(ref, n):
    rows = ref.shape[0] // n
    return [slice(c * rows, (c + 1) * rows) for c in range(n)]


def _proj_nt_kernel(x_ref, ssq_ref, w_ref, *refs, row_chunks):
    o_ref = refs[-1]
    w = w_ref[...].astype(BF16)
    for rs in _row_chunks(x_ref, row_chunks):
        acc = lax.dot_general(x_ref[rs, :], w, _NT, preferred_element_type=F32)
        o_ref[rs, :] = acc * lax.rsqrt(ssq_ref[rs, 0:1] * (1.0 / x_ref.shape[1]) + EPS)


def _after(in_specs, args, after):
    if after is not None:
        in_specs.append(pl.BlockSpec(memory_space=pl.ANY))
        args.append(after)


def _proj_nt(xg, ssq, wt, layer, row_block0, n_cols, tn, tm, after=None):
    m, k = xg.shape
    in_specs = [pl.BlockSpec((tm, k), lambda i, j: (i, 0)),
                pl.BlockSpec((tm, LANES), lambda i, j: (i, 0)),
                pl.BlockSpec((None, tn, k), lambda i, j: (layer, row_block0 + j, 0))]
    args = [xg, ssq, wt]
    _after(in_specs, args, after)
    return pl.pallas_call(
        functools.partial(_proj_nt_kernel, row_chunks=2 if tm % 32 == 0 else 1),
        grid=(m // tm, n_cols // tn),
        in_specs=in_specs,
        out_specs=pl.BlockSpec((tm, tn), lambda i, j: (i, j)),
        out_shape=jax.ShapeDtypeStruct((m, n_cols), F32),
        compiler_params=_params("parallel", "arbitrary"),
        name="proj_nt",
    )(*args)


def _qkv_kernel(x_ref, ssq_ref, wq_ref, wk_ref, wv_ref, *refs):
    q_ref, k_ref, v_ref = refs[-3:]
    x = x_ref[...]
    inv = _inv_rms(ssq_ref, x_ref.shape[1])
    for w_ref, o_ref in ((wq_ref, q_ref), (wk_ref, k_ref), (wv_ref, v_ref)):
        acc = lax.dot_general(x, w_ref[...].astype(BF16), _NT, preferred_element_type=F32)
        o_ref[...] = acc * inv


def _qkv_weight_specs(wt2d_rows, layer, tn, k_dim):
    base = layer * wt2d_rows + A_WIDTH + GATE_COLS
    specs = []
    for part in range(3):
        def index(*ids, part=part):
            j = ids[-1]
            return (pl.multiple_of(base + part * MIX_B + j * tn, SUBLANES), 0)
        specs.append(pl.BlockSpec((pl.Element(tn), pl.Element(k_dim)), index))
    return specs


def _qkv_prompt(xg, ssq, wt, layer, depth, n_rows, prev_k, prev_v, tm=QKV_ROW_TILE, tn=TN):
    k_dim = xg.shape[1]
    nb = MIX_B // tn
    n_all = wt.shape[1]
    kv_shape = jax.ShapeDtypeStruct((depth, n_rows, MIX_B), F32)
    in_specs = ([pl.BlockSpec((tm, k_dim), lambda i, j: (i, 0)),
                 pl.BlockSpec((tm, LANES), lambda i, j: (i, 0))]
                + _qkv_weight_specs(n_all, layer, tn, k_dim))
    wt2d = wt.reshape(depth * n_all, k_dim)
    args = [xg, ssq, wt2d, wt2d, wt2d]
    aliases = {}
    if prev_k is not None:
        in_specs += [pl.BlockSpec(memory_space=pl.ANY), pl.BlockSpec(memory_space=pl.ANY)]
        args += [prev_k, prev_v]
        aliases = {5: 1, 6: 2}
    return pl.pallas_call(
        _qkv_kernel,
        grid=(n_rows // tm, nb),
        in_specs=in_specs,
        out_specs=[pl.BlockSpec((tm, tn), lambda i, j: (i, j)),
                   pl.BlockSpec((None, tm, tn), lambda i, j: (layer, i, j)),
                   pl.BlockSpec((None, tm, tn), lambda i, j: (layer, i, j))],
        out_shape=[jax.ShapeDtypeStruct((n_rows, MIX_B), F32), kv_shape, kv_shape],
        input_output_aliases=aliases,
        compiler_params=_params("parallel", "arbitrary"),
        name="qkv_prompt",
    )(*args)


def _qkv_sample(xg, ssq, wt, layer, row_start, n_rows, tn=TN, after=None):
    k_dim = xg.shape[1]
    nb = MIX_B // tn
    depth, n_all, _ = wt.shape
    off = row_start // n_rows
    out = jax.ShapeDtypeStruct((n_rows, MIX_B), F32)
    wt2d = wt.reshape(depth * n_all, k_dim)
    in_specs = ([pl.BlockSpec((n_rows, k_dim), lambda j: (off, 0)),
                 pl.BlockSpec((n_rows, LANES), lambda j: (off, 0))]
                + _qkv_weight_specs(n_all, layer, tn, k_dim))
    args = [xg, ssq, wt2d, wt2d, wt2d]
    _after(in_specs, args, after)
    return pl.pallas_call(
        _qkv_kernel,
        grid=(nb,),
        in_specs=in_specs,
        out_specs=[pl.BlockSpec((n_rows, tn), lambda j: (0, j))] * 3,
        out_shape=[out, out, out],
        compiler_params=_params("arbitrary"),
        name="qkv_sample",
    )(*args)


def _ffn_up_kernel(x_ref, ssq_ref, wg_ref, wu_ref, wd_ref, o_ref, wd16_ref, *, row_chunks):
    wg = wg_ref[...].astype(BF16)
    wu = wu_ref[...].astype(BF16)
    for rs in _row_chunks(x_ref, row_chunks):
        x = x_ref[rs, :]
        inv = lax.rsqrt(ssq_ref[rs, 0:1] * (1.0 / x_ref.shape[1]) + EPS)
        g = jnp.dot(x, wg, preferred_element_type=F32) * inv
        u = jnp.dot(x, wu, preferred_element_type=F32) * inv
        o_ref[rs, :] = (g * jax.nn.sigmoid(g) * u).astype(o_ref.dtype)
    wd16_ref[...] = wd_ref[...].astype(wd16_ref.dtype)


def _ffn_up(xg, ssq, wg, wu, wd, layer, tm, tn=TN, wd_rows=WD_CAST_ROWS, row_chunks=2):
    m, k = xg.shape
    n = wg.shape[2]
    n_j = n // tn
    n_slabs = wd.shape[1] // wd_rows
    assert wd.shape[1] % wd_rows == 0 and n_slabs <= (m // tm) * n_j

    def slab(i, j):
        return jnp.minimum(i * n_j + j, n_slabs - 1)

    return pl.pallas_call(
        functools.partial(_ffn_up_kernel, row_chunks=row_chunks),
        grid=(m // tm, n_j),
        in_specs=[pl.BlockSpec((tm, k), lambda i, j: (i, 0)),
                  pl.BlockSpec((tm, LANES), lambda i, j: (i, 0)),
                  pl.BlockSpec((None, k, tn), lambda i, j: (layer, 0, j)),
                  pl.BlockSpec((None, k, tn), lambda i, j: (layer, 0, j)),
                  pl.BlockSpec((None, wd_rows, wd.shape[2]), lambda i, j: (layer, slab(i, j), 0))],
        out_specs=[pl.BlockSpec((tm, tn), lambda i, j: (i, j)),
                   pl.BlockSpec((wd_rows, wd.shape[2]), lambda i, j: (slab(i, j), 0))],
        out_shape=[jax.ShapeDtypeStruct((m, n), BF16),
                   jax.ShapeDtypeStruct(wd.shape[1:], BF16)],
        compiler_params=_params("arbitrary", "arbitrary"),
        name="ffn_up",
    )(xg, ssq, wg, wu, wd)


def _matmul_res_kernel(*refs, scale, emit_norm, n_lhs):
    a_refs, w_refs, (r_ref, *refs) = refs[:n_lhs], refs[n_lhs:2 * n_lhs], refs[2 * n_lhs:]
    acc = None
    for a_ref, w_ref in zip(a_refs, w_refs):
        prod = jnp.dot(a_ref[...], w_ref[...].astype(BF16), preferred_element_type=F32)
        acc = prod if acc is None else acc + prod
    x_new = r_ref[...] + scale * acc
    if not emit_norm:
        (o_ref,) = refs
        o_ref[...] = x_new
        return
    g_ref, o_ref, xg_ref, ssq_ref = refs
    o_ref[...] = x_new
    xg_ref[...] = (x_new * g_ref[...]).astype(xg_ref.dtype)
    part = jnp.broadcast_to(jnp.sum(x_new * x_new, axis=-1, keepdims=True), ssq_ref.shape)

    @pl.when(pl.program_id(1) == 0)
    def _():
        ssq_ref[...] = part

    @pl.when(pl.program_id(1) > 0)
    def _():
        ssq_ref[...] += part


def _matmul_res(lhs, w_specs, ws, res, scale, tm, tn, next_gain=None):
    m = lhs[0].shape[0]
    n = res.shape[1]
    emit_norm = next_gain is not None
    in_specs = ([pl.BlockSpec((tm, a.shape[1]), lambda i, j: (i, 0)) for a in lhs] + list(w_specs)
                + [pl.BlockSpec((tm, tn), lambda i, j: (i, j))])
    args = list(lhs) + list(ws) + [res]
    out_specs = [pl.BlockSpec((tm, tn), lambda i, j: (i, j))]
    out_shape = [jax.ShapeDtypeStruct((m, n), F32)]
    if emit_norm:
        in_specs.append(pl.BlockSpec((1, tn), lambda i, j: (0, j)))
        args.append(next_gain.reshape(1, n))
        out_specs += [pl.BlockSpec((tm, tn), lambda i, j: (i, j)),
                      pl.BlockSpec((tm, LANES), lambda i, j: (i, 0))]
        out_shape += [jax.ShapeDtypeStruct((m, n), BF16), jax.ShapeDtypeStruct((m, LANES), F32)]
    out = pl.pallas_call(
        functools.partial(_matmul_res_kernel, scale=scale, emit_norm=emit_norm, n_lhs=len(lhs)),
        grid=(m // tm, n // tn),
        in_specs=in_specs,
        out_specs=out_specs,
        out_shape=out_shape,
        compiler_params=_params("parallel", "arbitrary"),
        name="matmul_res",
    )(*args)
    return out if emit_norm else out[0]


def _place_rows_kernel(a_ref, b_ref, ya_in, yb_in, ya_ref, yb_ref):
    del ya_in, yb_in
    ya_ref[...] = a_ref[...].astype(ya_ref.dtype)
    yb_ref[...] = b_ref[...].astype(yb_ref.dtype)


def _place_rows(a, b, ya, yb, row_start):
    rows = a.shape[0]
    blk = row_start // rows
    return pl.pallas_call(
        _place_rows_kernel,
        grid=(1,),
        in_specs=[pl.BlockSpec(a.shape, lambda i: (0, 0)), pl.BlockSpec(b.shape, lambda i: (0, 0)),
                  pl.BlockSpec(memory_space=pl.ANY), pl.BlockSpec(memory_space=pl.ANY)],
        out_specs=[pl.BlockSpec(a.shape, lambda i: (blk, 0)), pl.BlockSpec(b.shape, lambda i: (blk, 0))],
        out_shape=[jax.ShapeDtypeStruct(ya.shape, ya.dtype), jax.ShapeDtypeStruct(yb.shape, yb.dtype)],
        input_output_aliases={2: 0, 3: 1},
        compiler_params=_params("arbitrary"),
        name="place_rows",
    )(a, b, ya, yb)


def _mlstm_kernel(q_ref, k_ref, v_ref, o_ref, g_ref, gb_ref, gh_ref, c0_ref, n0_ref, m0_ref,
                  y_ref, c_out_ref, n_out_ref, m_out_ref, c_scr, n_scr, m_scr, *, chunk, n_chunks):
    c_idx = pl.program_id(1)
    L = chunk

    @pl.when(c_idx == 0)
    def _():
        c_scr[...] = c0_ref[0]
        n_scr[...] = n0_ref[0]
        m_scr[...] = m0_ref[0]

    gates = g_ref[...] + gb_ref[...]
    lane = lax.broadcasted_iota(jnp.int32, gates.shape, 1)
    row = lax.broadcasted_iota(jnp.int32, (L, L), 0)
    col = lax.broadcasted_iota(jnp.int32, (L, L), 1)
    eye = row == col
    causal = col <= row
    anti = row <= col

    for head in range(N_HEADS_A):
        qs = slice(head * DK_A, (head + 1) * DK_A)
        vs = slice(head * DV_A, (head + 1) * DV_A)
        i_col = jnp.sum(jnp.where(lane == head, gates, 0.0), axis=1, keepdims=True)
        f_col = jnp.sum(jnp.where(lane == head + N_HEADS_A, gates, 0.0), axis=1, keepdims=True)
        lf_col = jax.nn.log_sigmoid(f_col)
        lf_row = jnp.sum(jnp.where(eye, lf_col, 0.0), axis=0, keepdims=True)
        i_row = jnp.sum(jnp.where(eye, i_col, 0.0), axis=0, keepdims=True)
        b_col = jnp.sum(jnp.where(causal, lf_row, 0.0), axis=1, keepdims=True)
        b_row = jnp.sum(jnp.where(anti, lf_col, 0.0), axis=0, keepdims=True)

        m_prev = m_scr[head]
        dmat = jnp.where(causal, b_col - b_row + i_row, -jnp.inf)
        inter = b_col + m_prev
        mt = jnp.maximum(inter, jnp.max(dmat, axis=1, keepdims=True))

        q32 = q_ref[:, qs]
        k32 = k_ref[:, qs] * (DK_A ** -0.5)
        q = q32.astype(BF16)
        k = k32.astype(BF16)
        v = v_ref[:, vs].astype(BF16)
        c_prev = c_scr[head]
        n_prev = n_scr[head]

        s = lax.dot_general(q, k, _NT, preferred_element_type=F32)
        w = s * jnp.exp(dmat - mt)
        a = jnp.exp(inter - mt)
        num = (jnp.dot(w.astype(BF16), v, preferred_element_type=F32)
               + a * jnp.dot(q, c_prev.astype(BF16), preferred_element_type=F32))
        den = (jnp.sum(w, axis=1, keepdims=True)
               + a * jnp.sum(q32 * n_prev, axis=1, keepdims=True))
        h = num / jnp.maximum(jnp.abs(den), jnp.exp(-mt))

        m_new = mt[L - 1:L, :]
        b_last = b_col[L - 1:L, :]
        g = jnp.exp(b_last - b_col + i_col - m_new)
        dec = jnp.exp(b_last + m_prev - m_new)
        gk = g * k32
        c_new = dec * c_prev + lax.dot_general(gk.astype(BF16), v, (((0,), (0,)), ((), ())),
                                               preferred_element_type=F32)
        n_new = dec * n_prev + jnp.sum(gk, axis=0, keepdims=True)
        c_scr[head] = c_new
        n_scr[head] = n_new
        m_scr[head] = m_new

        hn = h * lax.rsqrt(jnp.mean(h * h, axis=-1, keepdims=True) + EPS)
        y_ref[:, vs] = (hn * gh_ref[:, vs] * jax.nn.sigmoid(o_ref[:, vs])).astype(y_ref.dtype)

    @pl.when(c_idx == n_chunks - 1)
    def _():
        c_out_ref[0] = c_scr[...]
        n_out_ref[0] = n_scr[...]
        m_out_ref[0] = m_scr[...]


def _mlstm(pa, gates, gate_bias, g_head, layer, c0, n0, m0, row_start, batch, seq, chunk, out_dtype,
           out_rows=None):
    n_chunks = seq // chunk
    m_rows = batch * seq
    qk_w = N_HEADS_A * DK_A
    r0 = row_start // chunk
    h_a = N_HEADS_A

    def rows(b, c):
        return r0 + b * n_chunks + c

    def state_specs():
        return [pl.BlockSpec((1, h_a, DK_A, DV_A), lambda b, c: (b, 0, 0, 0)),
                pl.BlockSpec((1, h_a, 1, DK_A), lambda b, c: (b, 0, 0, 0)),
                pl.BlockSpec((1, h_a, 1, 1), lambda b, c: (b, 0, 0, 0))]

    kernel = functools.partial(_mlstm_kernel, chunk=chunk, n_chunks=n_chunks)
    y, c_fin, n_fin, m_fin = pl.pallas_call(
        kernel,
        grid=(batch, n_chunks),
        in_specs=[
            pl.BlockSpec((chunk, qk_w), lambda b, c: (rows(b, c), 0)),
            pl.BlockSpec((chunk, qk_w), lambda b, c: (rows(b, c), 1)),
            pl.BlockSpec((chunk, MIX_A), lambda b, c: (rows(b, c), 2 * qk_w // MIX_A)),
            pl.BlockSpec((chunk, MIX_A), lambda b, c: (rows(b, c), 2 * qk_w // MIX_A + 1)),
            pl.BlockSpec((chunk, LANES), lambda b, c: (rows(b, c), 0)),
            pl.BlockSpec((1, LANES), lambda b, c: (0, 0)),
            pl.BlockSpec((None, 1, MIX_A), lambda b, c: (layer, 0, 0)),
        ] + state_specs(),
        out_specs=[pl.BlockSpec((chunk, MIX_A), lambda b, c: (b * n_chunks + c, 0))] + state_specs(),
        out_shape=[
            jax.ShapeDtypeStruct((out_rows or m_rows, MIX_A), out_dtype),
            jax.ShapeDtypeStruct((batch, h_a, DK_A, DV_A), F32),
            jax.ShapeDtypeStruct((batch, h_a, 1, DK_A), F32),
            jax.ShapeDtypeStruct((batch, h_a, 1, 1), F32),
        ],
        scratch_shapes=[pltpu.VMEM((h_a, DK_A, DV_A), F32),
                        pltpu.VMEM((h_a, 1, DK_A), F32),
                        pltpu.VMEM((h_a, 1, 1), F32)],
        compiler_params=_params("parallel", "arbitrary"),
        name="mlstm",
    )(pa, pa, pa, pa, gates, gate_bias, g_head.reshape(g_head.shape[0], 1, MIX_A),
      c0, n0.reshape(batch, h_a, 1, DK_A), m0.reshape(batch, h_a, 1, 1))
    return (y, c_fin, n_fin.reshape(batch, h_a, DK_A), m_fin.reshape(batch, h_a))


def _t5_bucket(dist):
    max_exact = N_BUCKETS // 2
    d = np.maximum(dist, 1).astype(np.float64)
    large = max_exact + (np.log(d / max_exact) / math.log(MAX_DISTANCE / max_exact)
                         * (N_BUCKETS - max_exact)).astype(np.int32)
    large = np.minimum(large, N_BUCKETS - 1)
    return np.where(dist < max_exact, dist, large).astype(np.int32)


def _prompt_bias_rows(rel_bias):
    j = np.arange(2 * N_STEPS)
    valid = j <= N_STEPS
    rows = []
    for (_, r) in PATTERNS:
        bucket = _t5_bucket(np.clip(N_STEPS - j, 0, N_STEPS) * r)
        rows.append(jnp.where(valid[:, None], rel_bias[bucket].astype(F32), NEG_INF))
    return jnp.stack(rows, 0).transpose(2, 0, 1)[:, :, None, :]


def _attn_prompt_kernel(q_ref, k_ref, v_ref, bias_ref, y_ref, acc_scr, m_scr, l_scr, *, seq):
    scale = HEAD_DIM ** -0.5
    nw = N_STEPS
    n_blocks = seq // nw

    for p, (_, r) in enumerate(PATTERNS):
        nb = n_blocks // r

        def rows(g):
            c, n = divmod(g, nb)
            start = c + n * nw * r
            return pl.ds(start, nw, stride=r) if r > 1 else pl.ds(start, nw)

        tile = pltpu.roll(jnp.broadcast_to(bias_ref[0, p], (nw, 2 * nw)), 0, 1,
                          stride=1, stride_axis=0)
        tile_cur = tile[:, nw:]
        qs = [q_ref[rows(g), :].astype(BF16) for g in range(n_blocks)]
        ks = [k_ref[rows(g), :].astype(BF16) for g in range(n_blocks)]
        vs = [v_ref[rows(g), :].astype(BF16) for g in range(n_blocks)]
        for g in range(n_blocks):
            if g % nb:
                kk = jnp.concatenate([ks[g - 1], ks[g]], axis=0)
                vv = jnp.concatenate([vs[g - 1], vs[g]], axis=0)
                bias = tile
            else:
                kk, vv, bias = ks[g], vs[g], tile_cur
            s = lax.dot_general(qs[g], kk, _NT, preferred_element_type=F32)
            s = s * scale + bias
            m = jnp.max(s, axis=1, keepdims=True)
            e = jnp.exp(s - m)
            l = jnp.sum(e, axis=1, keepdims=True)
            acc = jnp.dot(e.astype(BF16), vv, preferred_element_type=F32)
            acc_scr[p, rows(g), :] = acc
            m_scr[p, rows(g), :] = jnp.broadcast_to(m, (nw, HEAD_DIM))
            l_scr[p, rows(g), :] = jnp.broadcast_to(l, (nw, HEAD_DIM))

    rows_per_step = MERGE_ROWS

    def merge(t, carry):
        rws = pl.ds(pl.multiple_of(t * rows_per_step, rows_per_step), rows_per_step)
        m0, m1, m2 = m_scr[0, rws, :], m_scr[1, rws, :], m_scr[2, rws, :]
        mx = jnp.maximum(jnp.maximum(m0, m1), m2)
        w0, w1, w2 = jnp.exp(m0 - mx), jnp.exp(m1 - mx), jnp.exp(m2 - mx)
        num = w0 * acc_scr[0, rws, :] + w1 * acc_scr[1, rws, :] + w2 * acc_scr[2, rws, :]
        den = w0 * l_scr[0, rws, :] + w1 * l_scr[1, rws, :] + w2 * l_scr[2, rws, :]
        y_ref[rws, :] = (num / den).astype(y_ref.dtype)
        return carry
    lax.fori_loop(0, seq // rows_per_step, merge, 0)


def _attn_prompt(q, k_all, v_all, layer, bias_rows, batch, seq, out_rows=None):
    n_pat = len(PATTERNS)
    kernel = functools.partial(_attn_prompt_kernel, seq=seq)
    return pl.pallas_call(
        kernel,
        grid=(batch, N_HEADS_B),
        in_specs=[
            pl.BlockSpec((seq, HEAD_DIM), lambda b, h: (b, h)),
            pl.BlockSpec((None, seq, HEAD_DIM), lambda b, h: (layer, b, h)),
            pl.BlockSpec((None, seq, HEAD_DIM), lambda b, h: (layer, b, h)),
            pl.BlockSpec((1, n_pat, 1, 2 * N_STEPS), lambda b, h: (h, 0, 0, 0)),
        ],
        out_specs=pl.BlockSpec((seq, HEAD_DIM), lambda b, h: (b, h)),
        out_shape=jax.ShapeDtypeStruct((out_rows or batch * seq, MIX_B), BF16),
        scratch_shapes=[pltpu.VMEM((n_pat, seq, HEAD_DIM), F32),
                        pltpu.VMEM((n_pat, seq, HEAD_DIM), F32),
                        pltpu.VMEM((n_pat, seq, HEAD_DIM), F32)],
        compiler_params=_params("parallel", "parallel"),
        name="attn_prompt",
    )(q, k_all, v_all, bias_rows)


def _sample_bias(rel_bias, wbuf, t_new):
    n = wbuf + t_new
    delta = np.arange(n)
    rows, valid_by_dist = [], []
    for (w, r) in PATTERNS:
        valid = (delta % r == 0) & (delta // r <= w // r)
        valid_by_dist.append(valid)
        b = rel_bias[_t5_bucket(np.minimum(delta, w))].astype(F32)
        rows.append(jnp.where(valid[:, None], b, NEG_INF))
    f = jnp.stack(rows, 0).transpose(2, 0, 1)
    f_rev = jnp.pad(f[:, :, ::-1], ((0, 0), (0, 0), (0, t_new)), constant_values=NEG_INF)
    bias_c = jnp.stack([f_rev[:, :, t_new - 1 - t: t_new - 1 - t + wbuf] for t in range(t_new)], axis=2)
    bias_n = jnp.stack([f_rev[:, :, n - 1 - t: n - 1 - t + t_new] for t in range(t_new)], axis=2)
    dist = np.arange(t_new)[:, None] + wbuf - np.arange(wbuf)[None, :]
    valid_c = np.stack([v[dist] for v in valid_by_dist], 0)
    return bias_c, bias_n, valid_c


def _far_chunks(valid_c, n_chunks, t_new):
    r_max = max(r for _, r in PATTERNS)
    wbuf = valid_c.shape[2]
    tokens = wbuf // n_chunks
    if r_max % t_new or tokens % r_max:
        return ()
    far = []
    for c in range(n_chunks):
        in_chunk = valid_c[:, :, c * tokens:(c + 1) * tokens]
        only_widest = all(not in_chunk[p].any() for p, (_, r) in enumerate(PATTERNS) if r != r_max)
        tok = np.arange(c * tokens, (c + 1) * tokens)
        inside = not in_chunk[:, :, (tok % r_max) >= t_new].any()
        if only_widest and inside:
            far.append(c)
    return tuple(far)


def _attn_sample_kernel(q_ref, kn_ref, vn_ref, kc_ref, vc_ref, kx_ref, vx_ref, bc_ref, bf_ref, bn_ref,
                        *refs, n_chunks, far_chunks, group_step):
    y_ref, ko_ref, vo_ref, m_scr, l_scr, acc_scr = refs[-6:]
    scale = HEAD_DIM ** -0.5
    n_pat = len(PATTERNS)
    chunk = pl.program_id(1)
    t_new = q_ref.shape[0]
    groups = kc_ref.shape[0]

    ko_ref[0:groups - 1] = kc_ref[1:groups]
    vo_ref[0:groups - 1] = vc_ref[1:groups]

    @pl.when(chunk < n_chunks - 1)
    def _():
        ko_ref[groups - 1] = kx_ref[0]
        vo_ref[groups - 1] = vx_ref[0]

    @pl.when(chunk == n_chunks - 1)
    def _():
        for h in range(N_HEADS_B):
            dst = pl.ds(h, t_new, stride=N_HEADS_B)
            ko_ref[groups - 1, dst, :] = kn_ref[:, h * HEAD_DIM:(h + 1) * HEAD_DIM]
            vo_ref[groups - 1, dst, :] = vn_ref[:, h * HEAD_DIM:(h + 1) * HEAD_DIM]

    @pl.when(chunk == 0)
    def _():
        m_scr[...] = jnp.full(m_scr.shape, NEG_INF, F32)
        l_scr[...] = jnp.zeros(l_scr.shape, F32)
        acc_scr[...] = jnp.zeros(acc_scr.shape, F32)

    def update(h, q, k, v, biases):
        s = lax.dot_general(q, k, _NT, preferred_element_type=F32) * scale
        zs = [s + b for b in biases]
        zmax = jnp.max(zs[0], axis=1, keepdims=True)
        for z in zs[1:]:
            zmax = jnp.maximum(zmax, jnp.max(z, axis=1, keepdims=True))
        m_old = m_scr[h]
        m_new = jnp.maximum(m_old, zmax)
        alpha = jnp.exp(m_old - m_new)
        pt = jnp.exp(zs[0] - m_new)
        for z in zs[1:]:
            pt = pt + jnp.exp(z - m_new)
        m_scr[h] = m_new
        l_scr[h] = alpha * l_scr[h] + jnp.sum(pt, axis=1, keepdims=True)
        acc_scr[h] = alpha * acc_scr[h] + jnp.dot(pt.astype(BF16), v, preferred_element_type=F32)

    def head_rows(ref, h, sparse):
        in_group = pl.ds(h, t_new, stride=N_HEADS_B)
        if sparse:
            x = ref[pl.ds(0, groups // group_step, stride=group_step), in_group, :]
        else:
            x = ref[:, in_group, :]
        return x.reshape(-1, HEAD_DIM).astype(BF16)

    def cached(sparse):
        for h in range(N_HEADS_B):
            q = q_ref[:, h * HEAD_DIM:(h + 1) * HEAD_DIM].astype(BF16)
            biases = [bf_ref[h]] if sparse else [bc_ref[h, p] for p in range(n_pat)]
            update(h, q, head_rows(kc_ref, h, sparse), head_rows(vc_ref, h, sparse), biases)

    if far_chunks:
        is_far = functools.reduce(jnp.logical_or, [chunk == c for c in far_chunks])
        pl.when(is_far)(lambda: cached(True))
        pl.when(jnp.logical_not(is_far))(lambda: cached(False))
    else:
        cached(False)

    @pl.when(chunk == n_chunks - 1)
    def _():
        for h in range(N_HEADS_B):
            cols = slice(h * HEAD_DIM, (h + 1) * HEAD_DIM)
            q = q_ref[:, cols].astype(BF16)
            update(h, q, kn_ref[:, cols].astype(BF16), vn_ref[:, cols].astype(BF16),
                   [bn_ref[h, p] for p in range(n_pat)])
            y_ref[:, cols] = (acc_scr[h] / l_scr[h]).astype(y_ref.dtype)


def _attn_sample(q, kn, vn, cache_k, cache_v, layer, bias_c, bias_n, valid_c, batch, t_new,
                 prev_k, prev_v, n_chunks=CACHE_CHUNKS):
    depth, _, all_groups, group_rows, _ = cache_k.shape
    groups = all_groups // n_chunks
    tokens = groups * t_new
    n_pat = len(PATTERNS)
    r_max = max(r for _, r in PATTERNS)
    far_pattern = [r for _, r in PATTERNS].index(r_max)
    far_chunks = _far_chunks(valid_c, n_chunks, t_new)
    group_step = r_max // t_new if far_chunks else 1
    n_heads, _, _, wbuf = bias_c.shape
    bias_far = bias_c[:, far_pattern].reshape(n_heads, t_new, wbuf // r_max, r_max)[..., :t_new]
    bias_far = bias_far.reshape(n_heads, t_new, wbuf // r_max * t_new)
    far_tokens = bias_far.shape[2] // n_chunks

    chunk_spec = pl.BlockSpec((None, None, groups, group_rows, HEAD_DIM), lambda b, c: (layer, b, c, 0, 0))
    next_spec = pl.BlockSpec((None, None, 1, group_rows, HEAD_DIM),
                             lambda b, c: (layer, b, jnp.minimum(c + 1, n_chunks - 1) * groups, 0, 0))
    row_spec = pl.BlockSpec((t_new, MIX_B), lambda b, c: (b, 0))
    in_specs = [row_spec, row_spec, row_spec, chunk_spec, chunk_spec, next_spec, next_spec,
                pl.BlockSpec((N_HEADS_B, n_pat, t_new, tokens), lambda b, c: (0, 0, 0, c)),
                pl.BlockSpec((N_HEADS_B, t_new, far_tokens), lambda b, c: (0, 0, c)),
                pl.BlockSpec((N_HEADS_B, n_pat, t_new, t_new), lambda b, c: (0, 0, 0, 0))]
    args = [q, kn, vn, cache_k, cache_v, cache_k, cache_v, bias_c, bias_far, bias_n]
    aliases = {}
    if prev_k is not None:
        in_specs += [pl.BlockSpec(memory_space=pl.ANY), pl.BlockSpec(memory_space=pl.ANY)]
        args += [prev_k, prev_v]
        aliases = {10: 1, 11: 2}
    cache_shape = jax.ShapeDtypeStruct(cache_k.shape, F32)
    kernel = functools.partial(_attn_sample_kernel, n_chunks=n_chunks, far_chunks=far_chunks,
                               group_step=group_step)
    return pl.pallas_call(
        kernel,
        grid=(batch, n_chunks),
        in_specs=in_specs,
        out_specs=[row_spec, chunk_spec, chunk_spec],
        out_shape=[jax.ShapeDtypeStruct((batch * t_new, MIX_B), F32), cache_shape, cache_shape],
        scratch_shapes=[pltpu.VMEM((N_HEADS_B, t_new, 1), F32),
                        pltpu.VMEM((N_HEADS_B, t_new, 1), F32),
                        pltpu.VMEM((N_HEADS_B, t_new, HEAD_DIM), F32)],
        input_output_aliases=aliases,
        compiler_params=_params("parallel", "arbitrary"),
        name="attn_sample",
    )(*args)


def kernel(x_prompt, x_sample, state_mlstm_C, state_mlstm_n, state_mlstm_m, cache_win_k, cache_win_v,
           g_ffn1, w_ffn1_gate, w_ffn1_up, w_ffn1_down, g_mix, w_in, b_igate, b_fgate, g_head, w_out,
           rel_bias, g_ffn2, w_ffn2_gate, w_ffn2_up, w_ffn2_down, g_final):
    bp, sp, d = x_prompt.shape
    bs, ts, _ = x_sample.shape
    depth = g_mix.shape[0]
    wbuf = cache_win_k.shape[2]
    mp, ms = bp * sp, bs * ts
    m_all = mp + ms
    d_ff = w_ffn1_gate.shape[2]
    tm_big = _row_tile(m_all, ROW_TILES_WIDE_K)
    tm_down = _row_tile(m_all, ROW_TILES_FF_K)

    zc = jnp.zeros((bp, N_HEADS_A, DK_A, DV_A), F32)
    zn = jnp.zeros((bp, N_HEADS_A, DK_A), F32)
    zm = jnp.zeros((bp, N_HEADS_A), F32)
    bias_rows = _prompt_bias_rows(rel_bias)
    bias_c, bias_n, valid_c = _sample_bias(rel_bias, wbuf, ts)
    cache_k = cache_win_k.reshape(depth, bs, wbuf // ts, ts * N_HEADS_B, HEAD_DIM)
    cache_v = cache_win_v.reshape(depth, bs, wbuf // ts, ts * N_HEADS_B, HEAD_DIM)
    w_in_t = jnp.swapaxes(w_in, 1, 2)

    def ffn(x, xg, ssq, wg, wu, wd, l, next_gain):
        a, wd16 = _ffn_up(xg, ssq, wg, wu, wd, l, tm_big)
        w_spec = pl.BlockSpec((d_ff, TN), lambda i, j: (0, j))
        return _matmul_res([a], [w_spec], [wd16], x, 0.5, tm_down, TN, next_gain)

    p_c, p_n, p_m, s_c, s_n, s_m = [], [], [], [], [], []
    pk = pv = sk = sv = None
    state = _gain_ssq(x_prompt.reshape(mp, d), g_ffn1[0], 0, m_all)
    x, xg, ssq = _gain_ssq(x_sample.reshape(ms, d), g_ffn1[0], mp, m_all, prev=state)
    for l in range(depth):
        x, xg, ssq = ffn(x, xg, ssq, w_ffn1_gate, w_ffn1_up, w_ffn1_down, l, g_mix[l])

        q_p, pk, pv = _qkv_prompt(xg, ssq, w_in_t, l, depth, mp, pk, pv)
        yb = _attn_prompt(q_p, pk, pv, l, bias_rows, bp, sp, out_rows=m_all)
        q_s, k_s, v_s = _qkv_sample(xg, ssq, w_in_t, l, mp, ms, after=yb)
        yb_s, sk, sv = _attn_sample(q_s, k_s, v_s, cache_k, cache_v, l, bias_c, bias_n, valid_c, bs, ts,
                                    sk, sv)

        pa = _proj_nt(xg, ssq, w_in_t, l, 0, A_WIDTH, TN_WIDE, tm_big, after=yb)
        gates = _proj_nt(xg, ssq, w_in_t, l, A_WIDTH // LANES, LANES, LANES, tm_big, after=yb)
        gate_bias = jnp.pad(jnp.concatenate([b_igate[l], b_fgate[l]]),
                            (0, LANES - GATE_COLS)).reshape(1, LANES)

        ya, c_, n_, m_ = _mlstm(pa, gates, gate_bias, g_head, l, zc, zn, zm,
                                0, bp, sp, PROMPT_CHUNK, BF16, out_rows=m_all)
        p_c.append(c_); p_n.append(n_); p_m.append(m_)
        ya_s, c_, n_, m_ = _mlstm(pa, gates, gate_bias, g_head, l, state_mlstm_C[l], state_mlstm_n[l],
                                  state_mlstm_m[l], mp, bs, ts, ts, F32)
        s_c.append(c_); s_n.append(n_); s_m.append(m_)

        ya, yb = _place_rows(ya_s, yb_s, ya, yb, mp)
        wo_specs = [pl.BlockSpec((None, MIX_A, TN), lambda i, j, l=l: (l, 0, j)),
                    pl.BlockSpec((None, MIX_B, TN), lambda i, j, l=l: (l, MIX_A // MIX_B, j))]
        x, xg, ssq = _matmul_res([ya, yb], wo_specs, [w_out, w_out], x, 1.0, tm_big, TN, g_ffn2[l])

        if l + 1 < depth:
            x, xg, ssq = ffn(x, xg, ssq, w_ffn2_gate, w_ffn2_up, w_ffn2_down, l, g_ffn1[l + 1])
        else:
            x = ffn(x, xg, ssq, w_ffn2_gate, w_ffn2_up, w_ffn2_down, l, None)

    yp = _rmsnorm(x, g_final, F32, 0, mp).reshape(bp, sp, d)
    ys = _rmsnorm(x, g_final, F32, mp, ms).reshape(bs, ts, d)
    return (yp, ys, jnp.stack(p_c), jnp.stack(p_n), jnp.stack(p_m),
            pk.reshape(depth, bp, sp, N_HEADS_B, HEAD_DIM), pv.reshape(depth, bp, sp, N_HEADS_B, HEAD_DIM),
            jnp.stack(s_c), jnp.stack(s_n), jnp.stack(s_m),
            sk.reshape(depth, bs, wbuf, N_HEADS_B, HEAD_DIM), sv.reshape(depth, bs, wbuf, N_HEADS_B, HEAD_DIM))
```
